```python
import math
import jax
import jax.numpy as jnp
from jax import lax
import numpy as np


D_MODEL = 1024
BATCH = 8
SEQ = 8192
DEPTH = 4

GRID_W = 64
CTX_LEN = 256
BLOCK = 128
EPS = 1e-6
ROPE_THETA = 10000.0

A_HEADS = 8
A_KV_HEADS = 2
A_HEAD_DIM = 128
A_Q = A_HEADS * A_HEAD_DIM
A_KV = A_KV_HEADS * A_HEAD_DIM
B_WIDTH = 512
B_CONV = 31
EVEN_IN = A_Q + 2 * A_KV + 2 * B_WIDTH
EVEN_OUT = A_Q + B_WIDTH

C_HEADS = 16
C_KV_HEADS = 2
C_HEAD_DIM = 64
C_WINDOW = 128
C_Q = C_HEADS * C_HEAD_DIM
C_KV = C_KV_HEADS * C_HEAD_DIM
D_INNER = 1024
D_HEAD_DIM = 64
D_HEADS = D_INNER // D_HEAD_DIM
D_STATE = 128
D_GROUPS = 2
D_CONV = 5
D_CHUNK = 128
D_XBC = D_INNER + 2 * D_GROUPS * D_STATE
ODD_CTX_LO = C_Q
ODD_CTX_HI = C_Q + 2 * C_KV + D_XBC + 2 * D_HEADS
ODD_IN = ODD_CTX_HI + D_INNER
ODD_OUT = C_Q + D_INNER

FFN_HIDDEN = -(-8 * D_MODEL // (3 * 256)) * 256

kernel_name = 'hybrid_prefix_diffusion_trunk'


def rms_norm(x, g):
    xf = x.astype(jnp.float32)
    y = xf * lax.rsqrt(jnp.mean(xf * xf, axis=-1, keepdims=True) + EPS)
    return (y * g.astype(jnp.float32)).astype(x.dtype)


def layer_norm(x, g, b):
    xf = x.astype(jnp.float32)
    mu = jnp.mean(xf, axis=-1, keepdims=True)
    var = jnp.mean(jnp.square(xf - mu), axis=-1, keepdims=True)
    y = (xf - mu) * lax.rsqrt(var + EPS) * g.astype(jnp.float32) + b.astype(jnp.float32)
    return y.astype(x.dtype)


def modulate(h, shift, scale):
    return h * (1.0 + scale) + shift


def heads(t, d):
    return t.reshape(t.shape[:-1] + (t.shape[-1] // d, d))


def rope_tables(L, head_dim, dtype):
    ROWS = L // GRID_W
    row = jnp.repeat(jnp.arange(ROWS), GRID_W).astype(jnp.float32)
    col = jnp.tile(jnp.arange(GRID_W), ROWS).astype(jnp.float32)
    quarter = head_dim // 4
    inv = ROPE_THETA ** (-jnp.arange(quarter, dtype=jnp.float32) / quarter)
    ar = row[:, None] * inv
    ac = col[:, None] * inv
    ang = jnp.concatenate([ar, ar, ac, ac], axis=-1)
    return jnp.cos(ang).astype(dtype)[:, None, :], jnp.sin(ang).astype(dtype)[:, None, :]


def apply_rope(x, cos, sin):
    q = x.shape[-1] // 4
    xs = x.reshape(x.shape[:-1] + (2, 2, q))
    rot = jnp.stack([-xs[..., 1, :], xs[..., 0, :]], axis=-2).reshape(x.shape)
    return x * cos + rot * sin


def softmax_with_sink(s, sink):
    if sink is None:
        return jax.nn.softmax(s, axis=-1)
    hkv, g = s.shape[1], s.shape[2]
    col = jnp.broadcast_to(sink.astype(jnp.float32).reshape(hkv, g)[None, :, :, None, None], s.shape[:-1] + (1,))
    return jax.nn.softmax(jnp.concatenate([s, col], axis=-1), axis=-1)[..., :-1]


def dense_block_attention(q, k, v, sink=None):
    b, Lq, hq, d = q.shape
    hkv = k.shape[2]
    g = hq // hkv
    nb = Lq // BLOCK
    qb = jnp.moveaxis((q * d ** -0.5).reshape(b, nb, BLOCK, hkv, g, d), 1, 0)

    def one(qi):
        s = jnp.einsum('bqkgd,bskd->bkgqs', qi, k).astype(jnp.float32)
        p = softmax_with_sink(s, sink).astype(v.dtype)
        return jnp.einsum('bkgqs,bskd->bqkgd', p, v)

    o = lax.map(one, qb)
    return jnp.moveaxis(o, 0, 1).reshape(b, Lq, hq * d)


def banded_block_attention(q, k, v, k_ctx, v_ctx, sink):
    b, L, hq, d = q.shape
    hkv = k.shape[2]
    g = hq // hkv
    nb = L // BLOCK
    W = C_WINDOW
    span = BLOCK + 2 * W
    qb = jnp.moveaxis((q * d ** -0.5).reshape(b, nb, BLOCK, hkv, g, d), 1, 0)
    kp = jnp.pad(k, ((0, 0), (W, W), (0, 0), (0, 0)))
    vp = jnp.pad(v, ((0, 0), (W, W), (0, 0), (0, 0)))

    def one(args):
        qi, n = args
        start = n * BLOCK
        kb = lax.dynamic_slice_in_dim(kp, start, span, axis=1)
        vb = lax.dynamic_slice_in_dim(vp, start, span, axis=1)
        qpos = start + jnp.arange(BLOCK)
        kpos = start - W + jnp.arange(span)
        band = (jnp.abs(qpos[:, None] - kpos[None, :]) <= W) & ((kpos >= 0) & (kpos < L))[None, :]
        s_lat = jnp.where(band, jnp.einsum('bqkgd,bskd->bkgqs', qi, kb).astype(jnp.float32), -jnp.inf)
        s_ctx = jnp.einsum('bqkgd,bskd->bkgqs', qi, k_ctx).astype(jnp.float32)
        p = softmax_with_sink(jnp.concatenate([s_lat, s_ctx], axis=-1), sink).astype(v.dtype)
        return (jnp.einsum('bkgqs,bskd->bqkgd', p[..., :span], vb)
                + jnp.einsum('bkgqs,bskd->bqkgd', p[..., span:], v_ctx))

    o = lax.map(one, (qb, jnp.arange(nb)))
    return jnp.moveaxis(o, 0, 1).reshape(b, L, hq * d)


def depthwise_conv(x, w, bias):
    K = w.shape[0]
    pad = K // 2
    y = lax.conv_general_dilated(x, w.astype(x.dtype)[:, None, :], window_strides=(1,),
                                 padding=[(pad, K - 1 - pad)],
                                 dimension_numbers=('NWC', 'WIO', 'NWC'),
                                 feature_group_count=x.shape[-1])
    return y + bias.astype(x.dtype)


def conv_module(u, dw_w, dw_b, cn_g, cn_b):
    a, gte = jnp.split(u, 2, axis=-1)
    hdn = depthwise_conv(a * jax.nn.sigmoid(gte), dw_w, dw_b)
    return jax.nn.silu(layer_norm(hdn, cn_g, cn_b))


def ssm_inputs(xbc, dtr, conv_w, conv_b, dt_bias):
    b, n, _ = xbc.shape
    xbc = jax.nn.silu(depthwise_conv(xbc, conv_w, conv_b))
    xs, Bm, Cm = jnp.split(xbc, [D_INNER, D_INNER + D_GROUPS * D_STATE], axis=-1)
    dt = jax.nn.softplus(dtr.astype(jnp.float32).reshape(b, n, 2, D_HEADS) + dt_bias.astype(jnp.float32))
    return (heads(xs, D_HEAD_DIM), Bm.reshape(b, n, D_GROUPS, D_STATE),
            Cm.reshape(b, n, D_GROUPS, D_STATE), dt)


def ssd_scan(xdt, a, Bm, Cm, h0):
    b, L, H, P = xdt.shape
    G, N = Bm.shape[2], Bm.shape[3]
    hg = H // G
    Q = D_CHUNK
    nc = L // Q
    xdt = xdt.reshape(b, nc, Q, G, hg, P)
    acs = jnp.cumsum(a.reshape(b, nc, Q, G, hg), axis=2)
    Bm = Bm.reshape(b, nc, Q, G, N).astype(jnp.float32)
    Cm = Cm.reshape(b, nc, Q, G, N).astype(jnp.float32)
    lower = jnp.tril(jnp.ones((Q, Q), dtype=bool))[:, :, None, None]
    seg = acs[:, :, :, None] - acs[:, :, None, :]
    decay = jnp.exp(jnp.where(lower, seg, -jnp.inf))
    cb = jnp.einsum('bcign,bcjgn->bcijg', Cm, Bm)
    y = jnp.einsum('bcijgh,bcjghp->bcighp', cb[..., None] * decay, xdt)
    xw = xdt * jnp.exp(acs[:, :, -1:] - acs)[..., None]
    states = jnp.einsum('bcjgn,bcjghp->bcghpn', Bm, xw)
    chunk_decay = jnp.exp(acs[:, :, -1])

    def step(h, inp):
        s, dcy = inp
        return h * dcy[..., None, None] + s, h

    h_T, h_in = lax.scan(step, h0.reshape(b, G, hg, P, N),
                         (jnp.moveaxis(states, 1, 0), jnp.moveaxis(chunk_decay, 1, 0)))
    h_in = jnp.moveaxis(h_in, 0, 1)
    y = y + jnp.einsum('bcign,bcghpn->bcighp', Cm, h_in) * jnp.exp(acs)[..., None]
    return y.reshape(b, L, H, P), h_T.reshape(b, H, P, N)


def ssd_final_state(xdt, a, Bm):
    b, L, H, P = xdt.shape
    G, N = Bm.shape[2], Bm.shape[3]
    acs = jnp.cumsum(a, axis=1)
    xw = (xdt * jnp.exp(acs[:, -1:] - acs)[..., None]).reshape(b, L, G, H // G, P)
    return jnp.einsum('blgn,blghp->bghpn', Bm.astype(jnp.float32), xw).reshape(b, H, P, N)


def swiglu(h, w_in, w_out):
    gte, up = jnp.split(h @ w_in, 2, axis=-1)
    return (jax.nn.silu(gte) * up) @ w_out


def even_mixer(h, hc, w_in, w_out, q_g, k_g, dw_w, dw_b, cn_g, cn_b, need_ctx):
    b, L, _ = h.shape
    cos, sin = rope_tables(L, A_HEAD_DIM, h.dtype)
    cuts = [A_Q, A_Q + A_KV, A_Q + 2 * A_KV]
    q, k, v, glu = jnp.split(h @ w_in, cuts, axis=-1)
    if need_ctx:
        qc, kc, vc, gluc = jnp.split(hc @ w_in, cuts, axis=-1)
    else:
        kc, vc = jnp.split(hc @ w_in[:, A_Q:A_Q + 2 * A_KV], [A_KV], axis=-1)
    q = apply_rope(rms_norm(heads(q, A_HEAD_DIM), q_g), cos, sin)
    k = apply_rope(rms_norm(heads(k, A_HEAD_DIM), k_g), cos, sin)
    kc = rms_norm(heads(kc, A_HEAD_DIM), k_g)
    v, vc = heads(v, A_HEAD_DIM), heads(vc, A_HEAD_DIM)
    attn = dense_block_attention(q, jnp.concatenate([kc, k], axis=1), jnp.concatenate([vc, v], axis=1))
    conv = conv_module(glu, dw_w, dw_b, cn_g, cn_b)
    out = jnp.concatenate([attn, conv], axis=-1) @ w_out
    out_c = None
    if need_ctx:
        qc = rms_norm(heads(qc, A_HEAD_DIM), q_g)
        out_c = jnp.concatenate([dense_block_attention(qc, kc, vc),
                                 conv_module(gluc, dw_w, dw_b, cn_g, cn_b)], axis=-1) @ w_out
    return out, out_c


def odd_mixer(h, hc, w_in, w_out, q_g, k_g, sink, conv_w, conv_b, dt_bias, A_log, D_skip, gnorm_g, need_ctx):
    b, L, _ = h.shape
    f32 = jnp.float32
    cos, sin = rope_tables(L, C_HEAD_DIM, h.dtype)
    cuts = [C_Q, C_Q + C_KV, C_Q + 2 * C_KV, C_Q + 2 * C_KV + D_XBC, ODD_CTX_HI]
    q, k, v, xbc, dtr, z = jnp.split(h @ w_in, cuts, axis=-1)
    if need_ctx:
        qc, kc, vc, xbcc, dtrc, zc = jnp.split(hc @ w_in, cuts, axis=-1)
    else:
        kc, vc, xbcc, dtrc = jnp.split(hc @ w_in[:, ODD_CTX_LO:ODD_CTX_HI],
                                       [C_KV, 2 * C_KV, 2 * C_KV + D_XBC], axis=-1)
    q = apply_rope(rms_norm(heads(q, C_HEAD_DIM), q_g), cos, sin)
    k = apply_rope(rms_norm(heads(k, C_HEAD_DIM), k_g), cos, sin)
    kc = rms_norm(heads(kc, C_HEAD_DIM), k_g)
    v, vc = heads(v, C_HEAD_DIM), heads(vc, C_HEAD_DIM)
    attn = banded_block_attention(q, k, v, kc, vc, sink)
    A = -jnp.exp(A_log.astype(f32))
    xs, Bm, Cm, dt = ssm_inputs(xbc, dtr, conv_w, conv_b, dt_bias)
    xsc, Bc, Cc, dtc = ssm_inputs(xbcc, dtrc, conv_w, conv_b, dt_bias)
    Dk = D_skip.astype(f32)[:, None]
    y = Dk * xs.astype(f32)
    yc = Dk * xsc.astype(f32) if need_ctx else None
    for dirn in (0, 1):
        rev = (lambda t: jnp.flip(t, axis=1)) if dirn else (lambda t: t)
        xdt_c = rev(xsc.astype(f32) * dtc[..., dirn, :, None])
        a_c = rev(dtc[..., dirn, :] * A[dirn])
        if need_ctx:
            y_c, h_c = ssd_scan(xdt_c, a_c, rev(Bc), rev(Cc), jnp.zeros((b, D_HEADS, D_HEAD_DIM, D_STATE), f32))
            yc = yc + rev(y_c)
        else:
            h_c = ssd_final_state(xdt_c, a_c, rev(Bc))
        y_l, _ = ssd_scan(rev(xs.astype(f32) * dt[..., dirn, :, None]), rev(dt[..., dirn, :] * A[dirn]),
                          rev(Bm), rev(Cm), h_c)
        y = y + rev(y_l)
    ssm = rms_norm(y.reshape(b, L, D_INNER) * jax.nn.silu(z.astype(f32)), gnorm_g).astype(h.dtype)
    out = jnp.concatenate([attn, ssm], axis=-1) @ w_out
    out_c = None
    if need_ctx:
        qc = rms_norm(heads(qc, C_HEAD_DIM), q_g)
        ssm_c = rms_norm(yc.reshape(b, -1, D_INNER) * jax.nn.silu(zc.astype(f32)), gnorm_g).astype(h.dtype)
        out_c = jnp.concatenate([dense_block_attention(qc, kc, vc, sink), ssm_c], axis=-1) @ w_out
    return out, out_c


def setup_inputs(seed: int = 0) -> dict:
    key = jax.random.key(seed)
    ks = iter(jax.random.split(key, 40))
    f32 = jnp.float32
    D = D_MODEL
    NE = (DEPTH + 1) // 2
    NO = DEPTH // 2

    def nrm(shape, scale):
        return jax.random.normal(next(ks), shape, f32) * scale

    inp = {}
    inp['x'] = nrm((BATCH, SEQ, D), 1.0)
    inp['c'] = nrm((BATCH, D), 1.0)
    inp['ctx'] = nrm((BATCH, CTX_LEN, D), 1.0)
    inp['c_ctx'] = nrm((D,), 1.0)
    inp['mod_w'] = nrm((DEPTH, D, 6 * D), 0.5 * D ** -0.5)
    inp['mod_b'] = nrm((DEPTH, 6 * D), 0.01)
    inp['norm1_g'] = 1.0 + nrm((DEPTH, D), 0.05)
    inp['norm2_g'] = 1.0 + nrm((DEPTH, D), 0.05)
    inp['ffn_w_in'] = nrm((DEPTH, D, 2 * FFN_HIDDEN), D ** -0.5)
    inp['ffn_w_out'] = nrm((DEPTH, FFN_HIDDEN, D), FFN_HIDDEN ** -0.5)
    inp['ev_w_in'] = nrm((NE, D, EVEN_IN), D ** -0.5)
    inp['ev_w_out'] = nrm((NE, EVEN_OUT, D), EVEN_OUT ** -0.5)
    inp['ev_q_g'] = 1.0 + nrm((NE, A_HEAD_DIM), 0.05)
    inp['ev_k_g'] = 1.0 + nrm((NE, A_HEAD_DIM), 0.05)
    inp['ev_dw_w'] = nrm((NE, B_CONV, B_WIDTH), B_CONV ** -0.5)
    inp['ev_dw_b'] = nrm((NE, B_WIDTH), 0.01)
    inp['ev_cn_g'] = 1.0 + nrm((NE, B_WIDTH), 0.05)
    inp['ev_cn_b'] = nrm((NE, B_WIDTH), 0.01)
    inp['od_w_in'] = nrm((NO, D, ODD_IN), D ** -0.5)
    inp['od_w_out'] = nrm((NO, ODD_OUT, D), ODD_OUT ** -0.5)
    inp['od_q_g'] = 1.0 + nrm((NO, C_HEAD_DIM), 0.05)
    inp['od_k_g'] = 1.0 + nrm((NO, C_HEAD_DIM), 0.05)
    inp['od_sink'] = nrm((NO, C_HEADS), 0.5)
    inp['od_conv_w'] = nrm((NO, D_CONV, D_XBC), D_CONV ** -0.5)
    inp['od_conv_b'] = nrm((NO, D_XBC), 0.01)
    dt0 = jnp.exp(jax.random.uniform(next(ks), (NO, 2, D_HEADS), f32, math.log(1e-3), math.log(1e-1)))
    inp['od_dt_bias'] = dt0 + jnp.log(-jnp.expm1(-dt0))
    inp['od_A_log'] = jnp.log(jax.random.uniform(next(ks), (NO, 2, D_HEADS), f32, 1.0, 16.0))
    inp['od_D'] = 1.0 + nrm((NO, D_HEADS), 0.1)
    inp['od_gnorm_g'] = 1.0 + nrm((NO, D_INNER), 0.05)
    return inp


def reference(x, c, ctx, c_ctx, mod_w, mod_b, norm1_g, norm2_g, ffn_w_in, ffn_w_out,
              ev_w_in, ev_w_out, ev_q_g, ev_k_g, ev_dw_w, ev_dw_b, ev_cn_g, ev_cn_b,
              od_w_in, od_w_out, od_q_g, od_k_g, od_sink, od_conv_w, od_conv_b,
              od_dt_bias, od_A_log, od_D, od_gnorm_g):
    cx = ctx
    for l in range(DEPTH):
        need_ctx = l < DEPTH - 1
        i = l // 2
        sh1, sc1, g1, sh2, sc2, g2 = jnp.split((jax.nn.silu(c) @ mod_w[l] + mod_b[l])[:, None, :], 6, axis=-1)
        csh1, csc1, cg1, csh2, csc2, cg2 = jnp.split(jax.nn.silu(c_ctx) @ mod_w[l] + mod_b[l], 6)
        h = modulate(rms_norm(x, norm1_g[l]), sh1, sc1)
        hc = modulate(rms_norm(cx, norm1_g[l]), csh1, csc1)
        if l % 2 == 0:
            o, oc = even_mixer(h, hc, ev_w_in[i], ev_w_out[i], ev_q_g[i], ev_k_g[i], ev_dw_w[i], ev_dw_b[i],
                               ev_cn_g[i], ev_cn_b[i], need_ctx)
        else:
            o, oc = odd_mixer(h, hc, od_w_in[i], od_w_out[i], od_q_g[i], od_k_g[i], od_sink[i], od_conv_w[i],
                              od_conv_b[i], od_dt_bias[i], od_A_log[i], od_D[i], od_gnorm_g[i], need_ctx)
        x = x + g1 * o
        x = x + g2 * swiglu(modulate(rms_norm(x, norm2_g[l]), sh2, sc2), ffn_w_in[l], ffn_w_out[l])
        if need_ctx:
            cx = cx + cg1 * oc
            cx = cx + cg2 * swiglu(modulate(rms_norm(cx, norm2_g[l]), csh2, csc2), ffn_w_in[l], ffn_w_out[l])
    return x
```

```python
import functools
import math

import jax
import jax.numpy as jnp
from jax import lax
from jax.experimental import pallas as pl
from jax.experimental.pallas import tpu as pltpu

F32 = jnp.float32
BF16 = jnp.bfloat16

EPS = 1e-6
ROPE_THETA = 10000.0
GRID_W = 64
LOG2E = math.log2(math.e)
NEG = -1e30

A_HEADS, A_KV_HEADS, A_HEAD_DIM = 8, 2, 128
B_WIDTH, B_CONV = 512, 31
C_HEADS, C_KV_HEADS, C_HEAD_DIM, C_WINDOW = 16, 2, 64, 128
D_INNER, D_HEAD_DIM, D_HEADS, D_STATE, D_GROUPS, D_CONV, D_CHUNK = 1024, 64, 16, 128, 2, 5, 128
A_Q, A_KV = A_HEADS * A_HEAD_DIM, A_KV_HEADS * A_HEAD_DIM
C_Q, C_KV = C_HEADS * C_HEAD_DIM, C_KV_HEADS * C_HEAD_DIM
D_XBC = D_INNER + 2 * D_GROUPS * D_STATE

LANES = 128
SUBLANES = 8
VMEM_BYTES = 64 * 1024 * 1024

TM = 256
MOD_ROWS = 16
EV_HALO = 16
OD_HALO = 8


def _cparams(sem, vmem_mb=None):
    kw = dict(dimension_semantics=sem)
    if vmem_mb is not None:
        kw["vmem_limit_bytes"] = vmem_mb * 1024 * 1024
    return pltpu.CompilerParams(**kw)


def _dot(a, b):
    return jnp.dot(a, b, preferred_element_type=F32)


def _dot_nt(a, b):
    return lax.dot_general(a, b, (((1,), (1,)), ((), ())), preferred_element_type=F32)


def _dot_tn(a, b):
    return lax.dot_general(a, b, (((0,), (0,)), ((), ())), preferred_element_type=F32)


def _silu(x):
    return x * jax.nn.sigmoid(x)


def _split2(x):
    hi = x.astype(BF16)
    lo = (x - hi.astype(F32)).astype(BF16)
    return hi, lo


def _prenorm(x, g, shift, scale):
    ms = jnp.mean(x * x, axis=-1, keepdims=True)
    return (x * lax.rsqrt(ms + EPS) * g) * (1.0 + scale) + shift


def _mod_kernel(c_ref, w_ref, b_ref, o_ref):
    a = _silu(c_ref[...]).astype(BF16)
    o_ref[...] = _dot(a, w_ref[...].astype(BF16)) + b_ref[...]


def _modulation(cc, mod_w, mod_b):
    depth, d, n = mod_w.shape
    tn = 6 * LANES * 2
    return pl.pallas_call(
        _mod_kernel,
        grid=(depth, n // tn),
        in_specs=[pl.BlockSpec((MOD_ROWS, d), lambda l, j: (0, 0)),
                  pl.BlockSpec((None, d, tn), lambda l, j: (l, 0, j)),
                  pl.BlockSpec((None, 1, tn), lambda l, j: (l, 0, j))],
        out_specs=pl.BlockSpec((None, MOD_ROWS, tn), lambda l, j: (l, 0, j)),
        out_shape=jax.ShapeDtypeStruct((depth, MOD_ROWS, n), F32),
        compiler_params=_cparams(("parallel", "parallel")),
        name="modulation",
    )(cc, mod_w, mod_b.reshape(depth, 1, n))


def _ev_in_kernel(x_ref, mod_ref, g_ref, w_ref, qg_ref, kg_ref, cos_ref, sin_ref,
                  q_ref, k_ref, v_ref, u_ref):
    hb = _prenorm(x_ref[...], g_ref[...], mod_ref[0:1, :], mod_ref[1:2, :]).astype(BF16)
    cos = cos_ref[...]
    sin = sin_ref[...]
    lane = lax.broadcasted_iota(jnp.int32, (1, LANES), 1)
    first = (lane % (A_HEAD_DIM // 2)) < (A_HEAD_DIM // 4)

    def norm_rope(t, g):
        y = t * lax.rsqrt(jnp.mean(t * t, axis=-1, keepdims=True) + EPS) * g
        rot = jnp.where(first, pltpu.roll(y, LANES - A_HEAD_DIM // 4, 1), pltpu.roll(y, A_HEAD_DIM // 4, 1))
        return y * cos + rot * sin

    qkv = _dot(hb, w_ref[:, 0:A_Q + 2 * A_KV])
    qg = qg_ref[...] * (A_HEAD_DIM ** -0.5 * LOG2E)
    for h in range(A_HEADS):
        sl = slice(h * LANES, (h + 1) * LANES)
        q_ref[:, sl] = norm_rope(qkv[:, sl], qg).astype(BF16)
    for h in range(A_KV_HEADS):
        sl = slice(h * LANES, (h + 1) * LANES)
        k_ref[:, sl] = norm_rope(qkv[:, A_Q + h * LANES:A_Q + (h + 1) * LANES], kg_ref[...]).astype(BF16)
    v_ref[...] = qkv[:, A_Q + A_KV:A_Q + 2 * A_KV].astype(BF16)
    glu = _dot(hb, w_ref[:, A_Q + 2 * A_KV:])
    u_ref[...] = glu[:, :B_WIDTH] * jax.nn.sigmoid(glu[:, B_WIDTH:])


def _row_specs(b_sz, nctx_tiles, d, layer):
    x_spec = pl.BlockSpec((None, TM, d), lambda b, i: (b, i, 0))
    mod_spec = pl.BlockSpec((None, None, 6, d), lambda b, i: (layer, jnp.where(i < nctx_tiles, b_sz, b), 0, 0))
    g_spec = pl.BlockSpec((None, 1, d), lambda b, i: (layer, 0, 0))
    return x_spec, mod_spec, g_spec


def _full(shape):
    nd = len(shape)
    return pl.BlockSpec(shape, lambda *_: (0,) * nd, pipeline_mode=pl.Buffered(1))


def _ev_in(xs, mod, norm_g, w, qg, kg, cos, sin, layer, nctx_tiles):
    b_sz, t, d = xs.shape
    x_spec, mod_spec, g_spec = _row_specs(b_sz, nctx_tiles, d, layer)
    tab = pl.BlockSpec((TM, LANES), lambda b, i: (i, 0))
    out = lambda n: pl.BlockSpec((None, TM, n), lambda b, i: (b, i, 0))
    return pl.pallas_call(
        _ev_in_kernel,
        grid=(b_sz, t // TM),
        in_specs=[x_spec, mod_spec, g_spec, _full(w.shape), _full(qg.shape), _full(kg.shape), tab, tab],
        out_specs=[out(A_Q), out(A_KV), out(A_KV), out(B_WIDTH)],
        out_shape=[jax.ShapeDtypeStruct((b_sz, t, A_Q), BF16),
                   jax.ShapeDtypeStruct((b_sz, t, A_KV), BF16),
                   jax.ShapeDtypeStruct((b_sz, t, A_KV), BF16),
                   jax.ShapeDtypeStruct((b_sz, t, B_WIDTH), F32)],
        compiler_params=_cparams(("parallel", "parallel"), 48),
        name="even_in_proj",
    )(xs, mod, norm_g, w, qg, kg, cos, sin)


def _ev_attn_kernel(q_ref, k_ref, v_ref, o_ref, *, tk, nk_ctx, nk_all, nctx_tiles):
    i = pl.program_id(2)
    tm = q_ref.shape[0]
    grp = A_HEADS // A_KV_HEADS
    q = jnp.concatenate([q_ref[:, j * LANES:(j + 1) * LANES] for j in range(grp)], axis=0)
    nk = jnp.where(i < nctx_tiles, nk_ctx, nk_all)

    def body(j, carry):
        m, l, acc = carry
        off = pl.multiple_of(j * tk, tk)
        s = _dot_nt(q, k_ref[pl.ds(off, tk), :])
        m_new = jnp.maximum(m, jnp.max(s, axis=-1, keepdims=True))
        p = jnp.exp2(s - m_new)
        alpha = jnp.exp2(m - m_new)
        l = alpha * l + jnp.sum(p, axis=-1, keepdims=True)
        acc = alpha * acc + _dot(p.astype(BF16), v_ref[pl.ds(off, tk), :])
        return m_new, l, acc

    init = (jnp.full((grp * tm, 1), NEG, F32), jnp.zeros((grp * tm, 1), F32), jnp.zeros((grp * tm, LANES), F32))
    _, l, acc = lax.fori_loop(0, nk, body, init)
    o = acc / l
    for j in range(grp):
        o_ref[:, j * LANES:(j + 1) * LANES] = o[j * tm:(j + 1) * tm].astype(BF16)


def _ev_attn(q, k, v, n_ctx):
    b_sz, t, _ = q.shape
    tk = TM
    grp_w = (A_HEADS // A_KV_HEADS) * A_HEAD_DIM
    kern = functools.partial(_ev_attn_kernel, tk=tk, nk_ctx=n_ctx // tk, nk_all=t // tk, nctx_tiles=n_ctx // TM)
    kv_spec = pl.BlockSpec((None, t, A_HEAD_DIM), lambda b, g, i: (b, 0, g))
    return pl.pallas_call(
        kern,
        grid=(b_sz, A_KV_HEADS, t // TM),
        in_specs=[pl.BlockSpec((None, TM, grp_w), lambda b, g, i: (b, i, g)), kv_spec, kv_spec],
        out_specs=pl.BlockSpec((None, TM, grp_w), lambda b, g, i: (b, i, g)),
        out_shape=jax.ShapeDtypeStruct((b_sz, t, A_Q), BF16),
        compiler_params=_cparams(("parallel", "parallel", "arbitrary"), 48),
        name="even_attention",
    )(q, k, v)


def _halo_flags(i, nctx_tiles, n_tiles):
    has_prev = jnp.logical_and(i != 0, i != nctx_tiles)
    has_next = jnp.logical_and(i != nctx_tiles - 1, i != n_tiles - 1)
    return has_prev, has_next


def _ev_conv_kernel(u_ref, up_ref, un_ref, w_ref, b_ref, g_ref, bb_ref, o_ref, buf, *, nctx_tiles, n_tiles, rb):
    i = pl.program_id(1)
    has_prev, has_next = _halo_flags(i, nctx_tiles, n_tiles)
    tm = u_ref.shape[0]
    buf[0:EV_HALO, :] = jnp.where(has_prev, up_ref[...], 0.0)
    buf[EV_HALO:EV_HALO + tm, :] = u_ref[...]
    buf[EV_HALO + tm:, :] = jnp.where(has_next, un_ref[...], 0.0)
    base = EV_HALO - B_CONV // 2
    for r in range(tm // rb):
        acc = jnp.zeros((rb, B_WIDTH), F32) + b_ref[...]
        for k in range(B_CONV):
            acc = acc + buf[pl.ds(r * rb + base + k, rb), :] * w_ref[k:k + 1, :]
        mu = jnp.mean(acc, axis=-1, keepdims=True)
        cen = acc - mu
        var = jnp.mean(cen * cen, axis=-1, keepdims=True)
        y = cen * lax.rsqrt(var + EPS) * g_ref[...] + bb_ref[...]
        o_ref[r * rb:(r + 1) * rb, :] = _silu(y).astype(BF16)


def _halo_specs(t, width, halo, layer_tiles):
    per = TM // halo
    last = t // halo - 1
    prev = pl.BlockSpec((None, halo, width), lambda b, i: (b, jnp.maximum(i * per - 1, 0), 0))
    nxt = pl.BlockSpec((None, halo, width), lambda b, i: (b, jnp.minimum((i + 1) * per, last), 0))
    return prev, nxt


def _ev_conv(u, w, bias, g, bb, n_ctx):
    b_sz, t, c = u.shape
    n_tiles = t // TM
    prev, nxt = _halo_specs(t, c, EV_HALO, n_tiles)
    kern = functools.partial(_ev_conv_kernel, nctx_tiles=n_ctx // TM, n_tiles=n_tiles, rb=32)
    return pl.pallas_call(
        kern,
        grid=(b_sz, n_tiles),
        in_specs=[pl.BlockSpec((None, TM, c), lambda b, i: (b, i, 0)), prev, nxt,
                  _full(w.shape), _full(bias.shape), _full(g.shape), _full(bb.shape)],
        out_specs=pl.BlockSpec((None, TM, c), lambda b, i: (b, i, 0)),
        out_shape=jax.ShapeDtypeStruct((b_sz, t, c), BF16),
        scratch_shapes=[pltpu.VMEM((TM + 2 * EV_HALO, c), F32)],
        compiler_params=_cparams(("parallel", "parallel")),
        name="even_conv_module",
    )(u, u, u, w, bias, g, bb)


def _out_ffn_kernel(x_ref, a1_ref, a2_ref, mod_ref, g_ref, wo1_ref, wo2_ref, wi_ref, wo_ref,
                    o_ref, act_ref, *, hid, chunk):
    o = _dot(a1_ref[...], wo1_ref[...]) + _dot(a2_ref[...], wo2_ref[...])
    x1 = x_ref[...] + mod_ref[2:3, :] * o
    h2 = _prenorm(x1, g_ref[...], mod_ref[3:4, :], mod_ref[4:5, :]).astype(BF16)
    for c in range(hid // chunk):
        gte = _dot(h2, wi_ref[:, c * chunk:(c + 1) * chunk])
        up = _dot(h2, wi_ref[:, hid + c * chunk:hid + (c + 1) * chunk])
        act_ref[:, c * chunk:(c + 1) * chunk] = (_silu(gte) * up).astype(BF16)
    o_ref[...] = x1 + mod_ref[5:6, :] * _dot(act_ref[...], wo_ref[...])


def _out_ffn(xs, a1, a2, mod, norm_g, wo1, wo2, wi, wo, layer, nctx_tiles):
    b_sz, t, d = xs.shape
    hid = wo.shape[0]
    x_spec, mod_spec, g_spec = _row_specs(b_sz, nctx_tiles, d, layer)
    row = lambda n: pl.BlockSpec((None, TM, n), lambda b, i: (b, i, 0))
    kern = functools.partial(_out_ffn_kernel, hid=hid, chunk=2 * LANES)
    return pl.pallas_call(
        kern,
        grid=(b_sz, t // TM),
        in_specs=[x_spec, row(a1.shape[-1]), row(a2.shape[-1]), mod_spec, g_spec,
                  _full(wo1.shape), _full(wo2.shape), _full(wi.shape), _full(wo.shape)],
        out_specs=x_spec,
        out_shape=jax.ShapeDtypeStruct(xs.shape, F32),
        scratch_shapes=[pltpu.VMEM((TM, hid), BF16)],
        input_output_aliases={0: 0},
        compiler_params=_cparams(("parallel", "parallel"), 56),
        name="out_proj_ffn",
    )(xs, a1, a2, mod, norm_g, wo1, wo2, wi, wo)


def _od_in_kernel(x_ref, mod_ref, g_ref, w_ref, qg_ref, kg_ref, cos_ref, sin_ref, bd_ref, dtb_ref,
                  q_ref, k_ref, v_ref, xbc_ref, z_ref, dt_ref):
    hb = _prenorm(x_ref[...], g_ref[...], mod_ref[0:1, :], mod_ref[1:2, :]).astype(BF16)
    cos = cos_ref[...]
    sin = sin_ref[...]
    bd = bd_ref[...]
    lane = lax.broadcasted_iota(jnp.int32, (1, LANES), 1)
    first = (lane % (C_HEAD_DIM // 2)) < (C_HEAD_DIM // 4)

    def norm_rope(t, g):
        hi, lo = _split2(t * t)
        ss = _dot(jnp.concatenate([hi, lo], axis=1), bd)
        y = t * lax.rsqrt(ss * (1.0 / C_HEAD_DIM) + EPS) * g
        rot = jnp.where(first, pltpu.roll(y, LANES - C_HEAD_DIM // 4, 1), pltpu.roll(y, C_HEAD_DIM // 4, 1))
        return y * cos + rot * sin

    nq = C_Q // LANES
    nk = 2 * C_KV // LANES
    qkv = _dot(hb, w_ref[:, 0:C_Q + 4 * C_KV])
    qg = qg_ref[...] * (C_HEAD_DIM ** -0.5 * LOG2E)
    for j in range(nq):
        sl = slice(j * LANES, (j + 1) * LANES)
        q_ref[:, sl] = norm_rope(qkv[:, sl], qg).astype(BF16)
    for j in range(nk):
        k_ref[:, j * LANES:(j + 1) * LANES] = norm_rope(qkv[:, C_Q + j * LANES:C_Q + (j + 1) * LANES], kg_ref[...]).astype(BF16)
    v_ref[...] = qkv[:, C_Q + 2 * C_KV:C_Q + 4 * C_KV].astype(BF16)
    rest = _dot(hb, w_ref[:, C_Q + 4 * C_KV:])
    xbc_ref[...] = rest[:, 0:D_XBC]
    z_ref[...] = rest[:, D_XBC:D_XBC + D_INNER]
    dtr = rest[:, D_XBC + D_INNER:] + dtb_ref[...]
    dt_ref[...] = jnp.maximum(dtr, 0.0) + jnp.log1p(jnp.exp(-jnp.abs(dtr)))


def _od_in(xs, mod, norm_g, w, qg, kg, cos, sin, bd, dtb, layer, nctx_tiles):
    b_sz, t, d = xs.shape
    x_spec, mod_spec, g_spec = _row_specs(b_sz, nctx_tiles, d, layer)
    tab = pl.BlockSpec((TM, LANES), lambda b, i: (i, 0))
    out = lambda n: pl.BlockSpec((None, TM, n), lambda b, i: (b, i, 0))
    return pl.pallas_call(
        _od_in_kernel,
        grid=(b_sz, t // TM),
        in_specs=[x_spec, mod_spec, g_spec, _full(w.shape), _full(qg.shape), _full(kg.shape), tab, tab,
                  _full(bd.shape), _full(dtb.shape)],
        out_specs=[out(C_Q), out(2 * C_KV), out(2 * C_KV), out(D_XBC), out(D_INNER), out(LANES)],
        out_shape=[jax.ShapeDtypeStruct((b_sz, t, C_Q), BF16),
                   jax.ShapeDtypeStruct((b_sz, t, 2 * C_KV), BF16),
                   jax.ShapeDtypeStruct((b_sz, t, 2 * C_KV), BF16),
                   jax.ShapeDtypeStruct((b_sz, t, D_XBC), F32),
                   jax.ShapeDtypeStruct((b_sz, t, D_INNER), F32),
                   jax.ShapeDtypeStruct((b_sz, t, LANES), F32)],
        compiler_params=_cparams(("parallel", "parallel"), 48),
        name="odd_in_proj",
    )(xs, mod, norm_g, w, qg, kg, cos, sin, bd, dtb)


def _od_attn_kernel(sink_ref, q_ref, k_ref, v_ref, o_ref, *, n_ctx, t, nctx_tiles):
    g = pl.program_id(1)
    i = pl.program_id(2)
    tm = q_ref.shape[0]
    span = tm + 2 * C_WINDOW
    grp = C_HEADS // C_KV_HEADS
    start = pl.multiple_of(jnp.clip(i * tm - C_WINDOW, n_ctx, t - span), LANES)
    kl = k_ref[pl.ds(start, span), :]
    vl = v_ref[pl.ds(start, span), :]
    kc = k_ref[0:n_ctx, :]
    vc = v_ref[0:n_ctx, :]
    qpos = i * tm + lax.broadcasted_iota(jnp.int32, (tm, span), 0)
    kpos = start + lax.broadcasted_iota(jnp.int32, (tm, span), 1)
    band = jnp.logical_and(jnp.abs(qpos - kpos) <= C_WINDOW, i >= nctx_tiles)
    low = lax.broadcasted_iota(jnp.int32, (1, LANES), 1) < C_HEAD_DIM
    zero = jnp.zeros((), BF16)
    kl_par = (jnp.where(low, kl, zero), jnp.where(low, zero, kl))
    kc_par = (jnp.where(low, kc, zero), jnp.where(low, zero, kc))
    for j in range(grp // 2):
        q2 = q_ref[:, j * LANES:(j + 1) * LANES]
        outs = []
        for par in range(2):
            sk = sink_ref[g * grp + 2 * j + par] * LOG2E
            s_l = jnp.where(band, _dot_nt(q2, kl_par[par]), NEG)
            s_c = _dot_nt(q2, kc_par[par])
            m = jnp.maximum(jnp.maximum(jnp.max(s_l, axis=-1, keepdims=True),
                                        jnp.max(s_c, axis=-1, keepdims=True)), sk)
            p_l = jnp.exp2(s_l - m)
            p_c = jnp.exp2(s_c - m)
            den = jnp.sum(p_l, axis=-1, keepdims=True) + jnp.sum(p_c, axis=-1, keepdims=True) + jnp.exp2(sk - m)
            outs.append((_dot(p_l.astype(BF16), vl) + _dot(p_c.astype(BF16), vc)) / den)
        o_ref[:, j * LANES:(j + 1) * LANES] = jnp.where(low, outs[0], outs[1]).astype(BF16)


def _od_attn(sink, q, k, v, n_ctx):
    b_sz, t, _ = q.shape
    grp_w = (C_HEADS // C_KV_HEADS) * C_HEAD_DIM
    kern = functools.partial(_od_attn_kernel, n_ctx=n_ctx, t=t, nctx_tiles=n_ctx // TM)
    kv_spec = pl.BlockSpec((None, t, LANES), lambda b, g, i: (b, 0, g))
    return pl.pallas_call(
        kern,
        grid=(b_sz, C_KV_HEADS, t // TM),
        in_specs=[pl.BlockSpec(memory_space=pltpu.SMEM),
                  pl.BlockSpec((None, TM, grp_w), lambda b, g, i: (b, i, g)), kv_spec, kv_spec],
        out_specs=pl.BlockSpec((None, TM, grp_w), lambda b, g, i: (b, i, g)),
        out_shape=jax.ShapeDtypeStruct((b_sz, t, C_Q), BF16),
        compiler_params=_cparams(("parallel", "parallel", "arbitrary")),
        name="odd_attention",
    )(sink, q, k, v)


def _od_conv_kernel(u_ref, up_ref, un_ref, w_ref, b_ref, xs_ref, bc_ref, buf, *, nctx_tiles, n_tiles, rb):
    i = pl.program_id(1)
    has_prev, has_next = _halo_flags(i, nctx_tiles, n_tiles)
    tm = u_ref.shape[0]
    buf[0:OD_HALO, :] = jnp.where(has_prev, up_ref[...], 0.0)
    buf[OD_HALO:OD_HALO + tm, :] = u_ref[...]
    buf[OD_HALO + tm:, :] = jnp.where(has_next, un_ref[...], 0.0)
    base = OD_HALO - D_CONV // 2
    for r in range(tm // rb):
        acc = jnp.zeros((rb, D_XBC), F32) + b_ref[...]
        for k in range(D_CONV):
            acc = acc + buf[pl.ds(r * rb + base + k, rb), :] * w_ref[k:k + 1, :]
        y = _silu(acc)
        xs_ref[r * rb:(r + 1) * rb, :] = y[:, 0:D_INNER]
        bc_ref[r * rb:(r + 1) * rb, :] = y[:, D_INNER:].astype(BF16)


def _od_conv(xbc, w, bias, n_ctx):
    b_sz, t, c = xbc.shape
    n_tiles = t // TM
    prev, nxt = _halo_specs(t, c, OD_HALO, n_tiles)
    kern = functools.partial(_od_conv_kernel, nctx_tiles=n_ctx // TM, n_tiles=n_tiles, rb=16)
    out = lambda n: pl.BlockSpec((None, TM, n), lambda b, i: (b, i, 0))
    return pl.pallas_call(
        kern,
        grid=(b_sz, n_tiles),
        in_specs=[pl.BlockSpec((None, TM, c), lambda b, i: (b, i, 0)), prev, nxt, _full(w.shape), _full(bias.shape)],
        out_specs=[out(D_INNER), out(c - D_INNER)],
        out_shape=[jax.ShapeDtypeStruct((b_sz, t, D_INNER), F32),
                   jax.ShapeDtypeStruct((b_sz, t, c - D_INNER), BF16)],
        scratch_shapes=[pltpu.VMEM((TM + 2 * OD_HALO, c), F32)],
        compiler_params=_cparams(("parallel", "parallel")),
        name="odd_ssd_conv",
    )(xbc, xbc, xbc, w, bias)


def _ssd_kernel(*refs, reverse, gated):
    if gated:
        xs_ref, bc_ref, dt_ref, av_ref, ex_ref, dsk_ref, y1_ref, z_ref, gn_ref, o_ref, h_ref = refs
    else:
        xs_ref, bc_ref, dt_ref, av_ref, ex_ref, dsk_ref, o_ref, h_ref = refs
    q = D_CHUNK
    gw = D_INNER // D_GROUPS
    hpg = D_HEADS // D_GROUPS
    col0 = D_HEADS if reverse else 0

    @pl.when(pl.program_id(1) == 0)
    def _():
        h_ref[...] = jnp.zeros_like(h_ref)

    dt = dt_ref[...]
    a = dt * av_ref[...]
    ii = lax.broadcasted_iota(jnp.int32, (q, q), 0)
    jj = lax.broadcasted_iota(jnp.int32, (q, q), 1)
    tri = (jj >= ii) if reverse else (jj <= ii)
    a1 = a.astype(BF16)
    r1 = a - a1.astype(F32)
    a2 = r1.astype(BF16)
    a3 = (r1 - a2.astype(F32)).astype(BF16)
    cs = _dot(jnp.where(tri, 1.0, 0.0).astype(BF16), jnp.concatenate([a1, a2, a3], axis=1))
    acs = cs[:, 0:LANES] + cs[:, LANES:2 * LANES] + cs[:, 2 * LANES:3 * LANES]
    edge = 0 if reverse else q - 1
    tot = acs[edge:edge + 1, :]

    def expand(w):
        hi, lo = _split2(w)
        return _dot(jnp.concatenate([hi, lo], axis=1), ex_ref[...])

    xs = xs_ref[...]
    w_y = expand(jnp.exp(acs))
    xdt = (xs * expand(dt)).astype(BF16)
    xw = (xs * expand(dt * jnp.exp(tot - acs))).astype(BF16)
    acs_t = acs.T
    bc = bc_ref[...]
    h_in = h_ref[...]
    hb = h_in.astype(BF16)
    low = lax.broadcasted_iota(jnp.int32, (1, LANES), 1) < D_HEAD_DIM
    ys = []
    for g in range(D_GROUPS):
        gs = slice(g * gw, (g + 1) * gw)
        b_g = bc[:, g * D_STATE:(g + 1) * D_STATE]
        c_g = bc[:, (D_GROUPS + g) * D_STATE:(D_GROUPS + g + 1) * D_STATE]
        cb = _dot_nt(c_g, b_g)
        inter = _dot(c_g, hb[:, gs]) * w_y[:, gs]
        for pr in range(hpg // 2):
            x2 = xdt[:, g * gw + pr * LANES:g * gw + (pr + 1) * LANES]
            outs = []
            for par in range(2):
                c = col0 + g * hpg + 2 * pr + par
                seg = acs[:, c:c + 1] - acs_t[c:c + 1, :]
                lmat = (jnp.exp(jnp.where(tri, seg, NEG)) * cb).astype(BF16)
                outs.append(_dot(lmat, x2))
            ys.append(jnp.where(low, outs[0], outs[1]) + inter[:, pr * LANES:(pr + 1) * LANES])
        h_ref[:, gs] = h_in[:, gs] * w_y[edge:edge + 1, gs] + _dot_tn(b_g, xw[:, gs])
    y = jnp.concatenate(ys, axis=1)
    if gated:
        gz = (y1_ref[...] + y) * _silu(z_ref[...])
        ms = jnp.mean(gz * gz, axis=-1, keepdims=True)
        o_ref[...] = (gz * lax.rsqrt(ms + EPS) * gn_ref[...]).astype(BF16)
    else:
        o_ref[...] = dsk_ref[...] * xs + y


def _ssd(xs, bc, dt, avec, ex, dsk, n_ctx, reverse, y1=None, z=None, gn=None):
    b_sz, t, _ = xs.shape
    q = D_CHUNK
    nc = t // q
    ncc = n_ctx // q
    if reverse:
        cmap = lambda b, s: (b, jnp.where(s < ncc, ncc - 1 - s, nc - 1 - (s - ncc)), 0)
    else:
        cmap = lambda b, s: (b, s, 0)
    row = lambda n: pl.BlockSpec((None, q, n), cmap)
    gated = y1 is not None
    args = [xs, bc, dt, avec, ex, dsk]
    specs = [row(D_INNER), row(bc.shape[-1]), row(LANES), _full(avec.shape), _full(ex.shape), _full(dsk.shape)]
    if gated:
        args += [y1, z, gn]
        specs += [row(D_INNER), row(D_INNER), _full(gn.shape)]
    return pl.pallas_call(
        functools.partial(_ssd_kernel, reverse=reverse, gated=gated),
        grid=(b_sz, nc),
        in_specs=specs,
        out_specs=row(D_INNER),
        out_shape=jax.ShapeDtypeStruct((b_sz, t, D_INNER), BF16 if gated else F32),
        scratch_shapes=[pltpu.VMEM((D_STATE, D_INNER), F32)],
        compiler_params=_cparams(("parallel", "arbitrary")),
        name="ssd_scan_bwd_gate" if gated else "ssd_scan_fwd",
    )(*args)


def _rope_tables(seq, n_ctx, head_dim):
    pos = jnp.arange(seq)
    row = (pos // GRID_W).astype(F32)
    col = (pos % GRID_W).astype(F32)
    quarter = head_dim // 4
    inv = ROPE_THETA ** (-jnp.arange(quarter, dtype=F32) / quarter)
    ar = row[:, None] * inv
    ac = col[:, None] * inv
    ang = jnp.concatenate([ar, ar, ac, ac], axis=-1)
    sign = jnp.where((jnp.arange(head_dim) % (head_dim // 2)) < quarter, -1.0, 1.0).astype(F32)
    cos = jnp.concatenate([jnp.ones((n_ctx, head_dim), F32), jnp.cos(ang)], axis=0)
    sin = jnp.concatenate([jnp.zeros((n_ctx, head_dim), F32), jnp.sin(ang) * sign], axis=0)
    rep = LANES // head_dim
    return jnp.tile(cos, (1, rep)), jnp.tile(sin, (1, rep))


def _dup_heads(w, head_dim):
    d, n = w.shape
    w = w.reshape(d, n // head_dim, 1, head_dim)
    return jnp.broadcast_to(w, (d, n // head_dim, 2, head_dim)).reshape(d, 2 * n)


def kernel(x, c, ctx, c_ctx, mod_w, mod_b, norm1_g, norm2_g, ffn_w_in, ffn_w_out, ev_w_in, ev_w_out, ev_q_g, ev_k_g, ev_dw_w, ev_dw_b, ev_cn_g, ev_cn_b, od_w_in, od_w_out, od_q_g, od_k_g, od_sink, od_conv_w, od_conv_b, od_dt_bias, od_A_log, od_D, od_gnorm_g):
    b_sz, seq, d = x.shape
    n_ctx = ctx.shape[1]
    depth = mod_w.shape[0]
    assert seq % TM == 0 and n_ctx % TM == 0 and seq >= TM + 2 * C_WINDOW and b_sz + 1 <= MOD_ROWS
    nctx_tiles = n_ctx // TM

    xs = jnp.concatenate([ctx, x], axis=1)
    cc = jnp.zeros((MOD_ROWS, d), F32).at[:b_sz].set(c).at[b_sz].set(c_ctx)
    mod = _modulation(cc, mod_w, mod_b).reshape(depth, MOD_ROWS, 6, d)
    g1 = norm1_g.reshape(depth, 1, d)
    g2 = norm2_g.reshape(depth, 1, d)

    cos_a, sin_a = _rope_tables(seq, n_ctx, A_HEAD_DIM)
    cos_c, sin_c = _rope_tables(seq, n_ctx, C_HEAD_DIM)
    lane = jnp.arange(LANES)
    bd = (lane[:, None] // C_HEAD_DIM == lane[None, :] // C_HEAD_DIM).astype(BF16)
    bd = jnp.concatenate([bd, bd], axis=0)
    head_of_lane = jnp.arange(D_INNER) // D_HEAD_DIM
    ex = []
    for dirn in range(2):
        e = (lane[:, None] == head_of_lane[None, :] + dirn * D_HEADS).astype(BF16)
        ex.append(jnp.concatenate([e, e], axis=0))

    for l in range(depth):
        i = l // 2
        wi = ffn_w_in[l].astype(BF16)
        wo = ffn_w_out[l].astype(BF16)
        if l % 2 == 0:
            w_in = ev_w_in[i].astype(BF16)
            w_out = ev_w_out[i].astype(BF16)
            q, k, v, u = _ev_in(xs, mod, g1, w_in, ev_q_g[i].reshape(1, -1), ev_k_g[i].reshape(1, -1),
                                cos_a, sin_a, l, nctx_tiles)
            a1 = _ev_attn(q, k, v, n_ctx)
            a2 = _ev_conv(u, ev_dw_w[i], ev_dw_b[i].reshape(1, -1), ev_cn_g[i].reshape(1, -1),
                          ev_cn_b[i].reshape(1, -1), n_ctx)
            wo1, wo2 = w_out[:A_Q], w_out[A_Q:]
        else:
            w = od_w_in[i]
            o_k, o_v, o_x, o_dt, o_z = C_Q, C_Q + C_KV, C_Q + 2 * C_KV, C_Q + 2 * C_KV + D_XBC, C_Q + 2 * C_KV + D_XBC + 2 * D_HEADS
            w_in = jnp.concatenate([
                w[:, :o_k], _dup_heads(w[:, o_k:o_v], C_HEAD_DIM), _dup_heads(w[:, o_v:o_x], C_HEAD_DIM),
                w[:, o_x:o_dt], w[:, o_z:], w[:, o_dt:o_z], jnp.zeros((d, LANES - 2 * D_HEADS), F32)], axis=1).astype(BF16)
            w_out = od_w_out[i].astype(BF16)
            pad = jnp.zeros((LANES - 2 * D_HEADS,), F32)
            dtb = jnp.concatenate([od_dt_bias[i].reshape(-1), pad]).reshape(1, LANES)
            avec = jnp.concatenate([-jnp.exp(od_A_log[i].astype(F32)).reshape(-1), pad]).reshape(1, LANES)
            dsk = jnp.repeat(od_D[i].astype(F32), D_HEAD_DIM).reshape(1, D_INNER)
            qg = jnp.tile(od_q_g[i], LANES // C_HEAD_DIM).reshape(1, LANES)
            kg = jnp.tile(od_k_g[i], LANES // C_HEAD_DIM).reshape(1, LANES)
            q, k, v, xbc, z, dt = _od_in(xs, mod, g1, w_in, qg, kg, cos_c, sin_c, bd, dtb, l, nctx_tiles)
            a1 = _od_attn(od_sink[i], q, k, v, n_ctx)
            xc, bc = _od_conv(xbc, od_conv_w[i], od_conv_b[i].reshape(1, -1), n_ctx)
            y1 = _ssd(xc, bc, dt, avec, ex[0], dsk, n_ctx, reverse=False)
            a2 = _ssd(xc, bc, dt, avec, ex[1], dsk, n_ctx, reverse=True, y1=y1, z=z,
                      gn=od_gnorm_g[i].reshape(1, -1))
            wo1, wo2 = w_out[:C_Q], w_out[C_Q:]
        xs = _out_ffn(xs, a1, a2, mod, g2, wo1, wo2, wi, wo, l, nctx_tiles)
    return xs[:, n_ctx:]
```

```python
import functools
import math

import jax
import jax.numpy as jnp
from jax import lax
from jax.experimental import pallas as pl
from jax.experimental.pallas import tpu as pltpu

F32 = jnp.float32
BF16 = jnp.bfloat16

EPS = 1e-6
ROPE_THETA = 10000.0
GRID_W = 64
LOG2E = math.log2(math.e)
NEG = -1e30

A_HEADS, A_KV_HEADS, A_HEAD_DIM = 8, 2, 128
B_WIDTH, B_CONV = 512, 31
C_HEADS, C_KV_HEADS, C_HEAD_DIM, C_WINDOW = 16, 2, 64, 128
D_INNER, D_HEAD_DIM, D_HEADS, D_STATE, D_GROUPS, D_CONV, D_CHUNK = 1024, 64, 16, 128, 2, 5, 128
A_Q, A_KV = A_HEADS * A_HEAD_DIM, A_KV_HEADS * A_HEAD_DIM
C_Q, C_KV = C_HEADS * C_HEAD_DIM, C_KV_HEADS * C_HEAD_DIM
D_XBC = D_INNER + 2 * D_GROUPS * D_STATE

LANES = 128
SUBLANES = 8
VMEM_BYTES = 64 * 1024 * 1024

TM = 256
MOD_ROWS = 16
EV_HALO = 16
OD_HALO = 8


def _cparams(sem, vmem_mb=None):
    kw = dict(dimension_semantics=sem)
    if vmem_mb is not None:
        kw["vmem_limit_bytes"] = vmem_mb * 1024 * 1024
    return pltpu.CompilerParams(**kw)


def _dot(a, b):
    return jnp.dot(a, b, preferred_element_type=F32)


def _dot_nt(a, b):
    return lax.dot_general(a, b, (((1,), (1,)), ((), ())), preferred_element_type=F32)


def _dot_tn(a, b):
    return lax.dot_general(a, b, (((0,), (0,)), ((), ())), preferred_element_type=F32)


def _silu(x):
    return x * jax.nn.sigmoid(x)


def _split2(x):
    hi = x.astype(BF16)
    lo = (x - hi.astype(F32)).astype(BF16)
    return hi, lo


def _prenorm(x, g, shift, scale):
    ms = jnp.mean(x * x, axis=-1, keepdims=True)
    return (x * lax.rsqrt(ms + EPS) * g) * (1.0 + scale) + shift


def _mod_kernel(c_ref, w_ref, b_ref, o_ref):
    a = _silu(c_ref[...]).astype(BF16)
    o_ref[...] = _dot(a, w_ref[...].astype(BF16)) + b_ref[...]


def _modulation(cc, mod_w, mod_b):
    depth, d, n = mod_w.shape
    tn = 6 * LANES * 2
    return pl.pallas_call(
        _mod_kernel,
        grid=(depth, n // tn),
        in_specs=[pl.BlockSpec((MOD_ROWS, d), lambda l, j: (0, 0)),
                  pl.BlockSpec((None, d, tn), lambda l, j: (l, 0, j)),
                  pl.BlockSpec((None, 1, tn), lambda l, j: (l, 0, j))],
        out_specs=pl.BlockSpec((None, MOD_ROWS, tn), lambda l, j: (l, 0, j)),
        out_shape=jax.ShapeDtypeStruct((depth, MOD_ROWS, n), F32),
        compiler_params=_cparams(("parallel", "parallel")),
        name="modulation",
    )(cc, mod_w, mod_b.reshape(depth, 1, n))


def _ev_in_kernel(x_ref, mod_ref, g_ref, w_ref, qg_ref, kg_ref, cos_ref, sin_ref,
                  q_ref, k_ref, vt_ref, u_ref):
    hb = _prenorm(x_ref[...], g_ref[...], mod_ref[0:1, :], mod_ref[1:2, :]).astype(BF16)
    cos = cos_ref[...]
    sin = sin_ref[...]
    lane = lax.broadcasted_iota(jnp.int32, (1, LANES), 1)
    first = (lane % (A_HEAD_DIM // 2)) < (A_HEAD_DIM // 4)

    def norm_rope(t, g):
        y = t * lax.rsqrt(jnp.mean(t * t, axis=-1, keepdims=True) + EPS) * g
        rot = jnp.where(first, pltpu.roll(y, LANES - A_HEAD_DIM // 4, 1), pltpu.roll(y, A_HEAD_DIM // 4, 1))
        return y * cos + rot * sin

    qkv = _dot(hb, w_ref[:, 0:A_Q + 2 * A_KV])
    qg = qg_ref[...] * (A_HEAD_DIM ** -0.5 * LOG2E)
    for h in range(A_HEADS):
        sl = slice(h * LANES, (h + 1) * LANES)
        q_ref[:, sl] = norm_rope(qkv[:, sl], qg).astype(BF16)
    for h in range(A_KV_HEADS):
        sl = slice(h * LANES, (h + 1) * LANES)
        k_ref[:, sl] = norm_rope(qkv[:, A_Q + h * LANES:A_Q + (h + 1) * LANES], kg_ref[...]).astype(BF16)
    for h in range(A_KV_HEADS):
        vt_ref[h] = qkv[:, A_Q + A_KV + h * LANES:A_Q + A_KV + (h + 1) * LANES].T.astype(BF16)
    glu = _dot(hb, w_ref[:, A_Q + 2 * A_KV:])
    u_ref[...] = glu[:, :B_WIDTH] * jax.nn.sigmoid(glu[:, B_WIDTH:])


def _row_specs(b_sz, nctx_tiles, d, layer):
    x_spec = pl.BlockSpec((None, TM, d), lambda b, i: (b, i, 0))
    mod_spec = pl.BlockSpec((None, None, 6, d), lambda b, i: (layer, jnp.where(i < nctx_tiles, b_sz, b), 0, 0))
    g_spec = pl.BlockSpec((None, 1, d), lambda b, i: (layer, 0, 0))
    return x_spec, mod_spec, g_spec


def _full(shape):
    nd = len(shape)
    return pl.BlockSpec(shape, lambda *_: (0,) * nd, pipeline_mode=pl.Buffered(1))


def _ev_in(xs, mod, norm_g, w, qg, kg, cos, sin, layer, nctx_tiles):
    b_sz, t, d = xs.shape
    x_spec, mod_spec, g_spec = _row_specs(b_sz, nctx_tiles, d, layer)
    tab = pl.BlockSpec((TM, LANES), lambda b, i: (i, 0))
    out = lambda n: pl.BlockSpec((None, TM, n), lambda b, i: (b, i, 0))
    return pl.pallas_call(
        _ev_in_kernel,
        grid=(b_sz, t // TM),
        in_specs=[x_spec, mod_spec, g_spec, _full(w.shape), _full(qg.shape), _full(kg.shape), tab, tab],
        out_specs=[out(A_Q), out(A_KV),
                   pl.BlockSpec((None, A_KV_HEADS, None, A_HEAD_DIM, TM), lambda b, i: (b, 0, i, 0, 0)),
                   out(B_WIDTH)],
        out_shape=[jax.ShapeDtypeStruct((b_sz, t, A_Q), BF16),
                   jax.ShapeDtypeStruct((b_sz, t, A_KV), BF16),
                   jax.ShapeDtypeStruct((b_sz, A_KV_HEADS, t // TM, A_HEAD_DIM, TM), BF16),
                   jax.ShapeDtypeStruct((b_sz, t, B_WIDTH), F32)],
        compiler_params=_cparams(("parallel", "parallel"), 48),
        name="even_in_proj",
    )(xs, mod, norm_g, w, qg, kg, cos, sin)


def _ev_attn_kernel(q_ref, k_ref, vt_ref, o_ref, acc_ref, s_ref, *, kb_per_step, n_ctx, nctx_tiles):
    i = pl.program_id(2)
    tm = q_ref.shape[0]
    kb = vt_ref.shape[-1]
    grp = A_HEADS // A_KV_HEADS
    m_rows = grp * tm
    q = jnp.concatenate([q_ref[:, j * LANES:(j + 1) * LANES] for j in range(grp)], axis=0)

    def softmax_pv(s, vt, m, l, first):
        m_new = jnp.maximum(m, jnp.max(s, axis=0, keepdims=True))
        p = jnp.exp2(s - m_new)
        alpha = jnp.exp2(m - m_new)
        l_new = alpha * l + jnp.sum(p, axis=0, keepdims=True)
        pv = _dot(vt, p.astype(BF16))
        acc_ref[...] = pv if first else acc_ref[...] * alpha + pv
        return m_new, l_new

    def finish(l):
        o = (acc_ref[...] / l).T
        for j in range(grp):
            o_ref[:, j * LANES:(j + 1) * LANES] = o[j * tm:(j + 1) * tm].astype(BF16)

    nb_ctx = n_ctx // kb
    m, l = softmax_pv(_dot_nt(k_ref[0:n_ctx, :], q), jnp.concatenate([vt_ref[j] for j in range(nb_ctx)], axis=1),
                      jnp.full((1, m_rows), NEG, F32), jnp.zeros((1, m_rows), F32), True)

    @pl.when(i < nctx_tiles)
    def _():
        finish(l)

    kps = kb_per_step

    def scores(blk, slot):
        off = pl.multiple_of(blk * kb, kb)
        s_ref[slot] = _dot_nt(k_ref[pl.ds(off, kps * kb), :], q)

    def consume(blk, slot, carry):
        vt = jnp.concatenate([vt_ref[blk + r] for r in range(kps)], axis=1)
        return softmax_pv(s_ref[slot], vt, *carry, False)

    n_pairs = (vt_ref.shape[0] - nb_ctx) // (2 * kps)

    @pl.when(i >= nctx_tiles)
    def _():
        scores(nb_ctx, 0)

        def body(jj, carry):
            blk = nb_ctx + 2 * kps * jj
            scores(blk + kps, 1)
            carry = consume(blk, 0, carry)
            scores(blk + 2 * kps, 0)
            return consume(blk + kps, 1, carry)

        carry = lax.fori_loop(0, n_pairs - 1, body, (m, l))
        blk = nb_ctx + 2 * kps * (n_pairs - 1)
        scores(blk + kps, 1)
        carry = consume(blk, 0, carry)
        _, l_fin = consume(blk + kps, 1, carry)
        finish(l_fin)


def _ev_attn(q, k, vt, n_ctx):
    b_sz, t, _ = q.shape
    nblk, kb = vt.shape[2], vt.shape[4]
    kb_per_step = 4
    assert (nblk - n_ctx // kb) % (2 * kb_per_step) == 0
    grp = A_HEADS // A_KV_HEADS
    grp_w = grp * A_HEAD_DIM
    kern = functools.partial(_ev_attn_kernel, kb_per_step=kb_per_step, n_ctx=n_ctx, nctx_tiles=n_ctx // TM)
    return pl.pallas_call(
        kern,
        grid=(b_sz, A_KV_HEADS, t // TM),
        in_specs=[pl.BlockSpec((None, TM, grp_w), lambda b, g, i: (b, i, g)),
                  pl.BlockSpec((None, t, A_HEAD_DIM), lambda b, g, i: (b, 0, g)),
                  pl.BlockSpec((None, None, nblk, A_HEAD_DIM, kb), lambda b, g, i: (b, g, 0, 0, 0))],
        out_specs=pl.BlockSpec((None, TM, grp_w), lambda b, g, i: (b, i, g)),
        out_shape=jax.ShapeDtypeStruct((b_sz, t, A_Q), BF16),
        scratch_shapes=[pltpu.VMEM((A_HEAD_DIM, grp * TM), F32),
                        pltpu.VMEM((2, kb_per_step * kb, grp * TM), F32)],
        compiler_params=_cparams(("parallel", "parallel", "arbitrary"), 48),
        name="even_attention",
    )(q, k, vt)


def _halo_flags(i, nctx_tiles, n_tiles):
    has_prev = jnp.logical_and(i != 0, i != nctx_tiles)
    has_next = jnp.logical_and(i != nctx_tiles - 1, i != n_tiles - 1)
    return has_prev, has_next


def _ev_conv_kernel(u_ref, up_ref, un_ref, w_ref, b_ref, g_ref, bb_ref, o_ref, buf, *, nctx_tiles, n_tiles, rb):
    i = pl.program_id(1)
    has_prev, has_next = _halo_flags(i, nctx_tiles, n_tiles)
    tm = u_ref.shape[0]
    buf[0:EV_HALO, :] = jnp.where(has_prev, up_ref[...], 0.0)
    buf[EV_HALO:EV_HALO + tm, :] = u_ref[...]
    buf[EV_HALO + tm:, :] = jnp.where(has_next, un_ref[...], 0.0)
    base = EV_HALO - B_CONV // 2
    for r in range(tm // rb):
        acc = jnp.zeros((rb, B_WIDTH), F32) + b_ref[...]
        for k in range(B_CONV):
            acc = acc + buf[pl.ds(r * rb + base + k, rb), :] * w_ref[k:k + 1, :]
        mu = jnp.mean(acc, axis=-1, keepdims=True)
        cen = acc - mu
        var = jnp.mean(cen * cen, axis=-1, keepdims=True)
        y = cen * lax.rsqrt(var + EPS) * g_ref[...] + bb_ref[...]
        o_ref[r * rb:(r + 1) * rb, :] = _silu(y).astype(BF16)


def _halo_specs(t, width, halo, layer_tiles):
    per = TM // halo
    last = t // halo - 1
    prev = pl.BlockSpec((None, halo, width), lambda b, i: (b, jnp.maximum(i * per - 1, 0), 0))
    nxt = pl.BlockSpec((None, halo, width), lambda b, i: (b, jnp.minimum((i + 1) * per, last), 0))
    return prev, nxt


def _ev_conv(u, w, bias, g, bb, n_ctx):
    b_sz, t, c = u.shape
    n_tiles = t // TM
    prev, nxt = _halo_specs(t, c, EV_HALO, n_tiles)
    kern = functools.partial(_ev_conv_kernel, nctx_tiles=n_ctx // TM, n_tiles=n_tiles, rb=32)
    return pl.pallas_call(
        kern,
        grid=(b_sz, n_tiles),
        in_specs=[pl.BlockSpec((None, TM, c), lambda b, i: (b, i, 0)), prev, nxt,
                  _full(w.shape), _full(bias.shape), _full(g.shape), _full(bb.shape)],
        out_specs=pl.BlockSpec((None, TM, c), lambda b, i: (b, i, 0)),
        out_shape=jax.ShapeDtypeStruct((b_sz, t, c), BF16),
        scratch_shapes=[pltpu.VMEM((TM + 2 * EV_HALO, c), F32)],
        compiler_params=_cparams(("parallel", "parallel")),
        name="even_conv_module",
    )(u, u, u, w, bias, g, bb)


def _out_ffn_kernel(x_ref, a1_ref, a2_ref, mod_ref, g_ref, wo1_ref, wo2_ref, wi_ref, wo_ref,
                    o_ref, act_ref, *, hid, chunk):
    o = _dot(a1_ref[...], wo1_ref[...]) + _dot(a2_ref[...], wo2_ref[...])
    x1 = x_ref[...] + mod_ref[2:3, :] * o
    h2 = _prenorm(x1, g_ref[...], mod_ref[3:4, :], mod_ref[4:5, :]).astype(BF16)
    for c in range(hid // chunk):
        gte = _dot(h2, wi_ref[:, c * chunk:(c + 1) * chunk])
        up = _dot(h2, wi_ref[:, hid + c * chunk:hid + (c + 1) * chunk])
        act_ref[:, c * chunk:(c + 1) * chunk] = (_silu(gte) * up).astype(BF16)
    o_ref[...] = x1 + mod_ref[5:6, :] * _dot(act_ref[...], wo_ref[...])


def _out_ffn(xs, a1, a2, mod, norm_g, wo1, wo2, wi, wo, layer, nctx_tiles):
    b_sz, t, d = xs.shape
    hid = wo.shape[0]
    x_spec, mod_spec, g_spec = _row_specs(b_sz, nctx_tiles, d, layer)
    row = lambda n: pl.BlockSpec((None, TM, n), lambda b, i: (b, i, 0))
    kern = functools.partial(_out_ffn_kernel, hid=hid, chunk=2 * LANES)
    return pl.pallas_call(
        kern,
        grid=(b_sz, t // TM),
        in_specs=[x_spec, row(a1.shape[-1]), row(a2.shape[-1]), mod_spec, g_spec,
                  _full(wo1.shape), _full(wo2.shape), _full(wi.shape), _full(wo.shape)],
        out_specs=x_spec,
        out_shape=jax.ShapeDtypeStruct(xs.shape, F32),
        scratch_shapes=[pltpu.VMEM((TM, hid), BF16)],
        input_output_aliases={0: 0},
        compiler_params=_cparams(("parallel", "parallel"), 56),
        name="out_proj_ffn",
    )(xs, a1, a2, mod, norm_g, wo1, wo2, wi, wo)


def _od_in_kernel(x_ref, mod_ref, g_ref, w_ref, qg_ref, kg_ref, cos_ref, sin_ref, bd_ref, dtb_ref,
                  q_ref, k_ref, v_ref, xbc_ref, z_ref, dt_ref):
    hb = _prenorm(x_ref[...], g_ref[...], mod_ref[0:1, :], mod_ref[1:2, :]).astype(BF16)
    cos = cos_ref[...]
    sin = sin_ref[...]
    bd = bd_ref[...]
    lane = lax.broadcasted_iota(jnp.int32, (1, LANES), 1)
    first = (lane % (C_HEAD_DIM // 2)) < (C_HEAD_DIM // 4)

    def norm_rope(t, g):
        hi, lo = _split2(t * t)
        ss = _dot(jnp.concatenate([hi, lo], axis=1), bd)
        y = t * lax.rsqrt(ss * (1.0 / C_HEAD_DIM) + EPS) * g
        rot = jnp.where(first, pltpu.roll(y, LANES - C_HEAD_DIM // 4, 1), pltpu.roll(y, C_HEAD_DIM // 4, 1))
        return y * cos + rot * sin

    nq = C_Q // LANES
    nk = 2 * C_KV // LANES
    qkv = _dot(hb, w_ref[:, 0:C_Q + 4 * C_KV])
    qg = qg_ref[...] * (C_HEAD_DIM ** -0.5 * LOG2E)
    for j in range(nq):
        sl = slice(j * LANES, (j + 1) * LANES)
        q_ref[:, sl] = norm_rope(qkv[:, sl], qg).astype(BF16)
    for j in range(nk):
        k_ref[:, j * LANES:(j + 1) * LANES] = norm_rope(qkv[:, C_Q + j * LANES:C_Q + (j + 1) * LANES], kg_ref[...]).astype(BF16)
    v_ref[...] = qkv[:, C_Q + 2 * C_KV:C_Q + 4 * C_KV].astype(BF16)
    rest = _dot(hb, w_ref[:, C_Q + 4 * C_KV:])
    xbc_ref[...] = rest[:, 0:D_XBC]
    z_ref[...] = rest[:, D_XBC:D_XBC + D_INNER]
    dtr = rest[:, D_XBC + D_INNER:] + dtb_ref[...]
    dt_ref[...] = jnp.maximum(dtr, 0.0) + jnp.log1p(jnp.exp(-jnp.abs(dtr)))


def _od_in(xs, mod, norm_g, w, qg, kg, cos, sin, bd, dtb, layer, nctx_tiles):
    b_sz, t, d = xs.shape
    x_spec, mod_spec, g_spec = _row_specs(b_sz, nctx_tiles, d, layer)
    tab = pl.BlockSpec((TM, LANES), lambda b, i: (i, 0))
    out = lambda n: pl.BlockSpec((None, TM, n), lambda b, i: (b, i, 0))
    return pl.pallas_call(
        _od_in_kernel,
        grid=(b_sz, t // TM),
        in_specs=[x_spec, mod_spec, g_spec, _full(w.shape), _full(qg.shape), _full(kg.shape), tab, tab,
                  _full(bd.shape), _full(dtb.shape)],
        out_specs=[out(C_Q), out(2 * C_KV), out(2 * C_KV), out(D_XBC), out(D_INNER), out(LANES)],
        out_shape=[jax.ShapeDtypeStruct((b_sz, t, C_Q), BF16),
                   jax.ShapeDtypeStruct((b_sz, t, 2 * C_KV), BF16),
                   jax.ShapeDtypeStruct((b_sz, t, 2 * C_KV), BF16),
                   jax.ShapeDtypeStruct((b_sz, t, D_XBC), F32),
                   jax.ShapeDtypeStruct((b_sz, t, D_INNER), F32),
                   jax.ShapeDtypeStruct((b_sz, t, LANES), F32)],
        compiler_params=_cparams(("parallel", "parallel"), 48),
        name="odd_in_proj",
    )(xs, mod, norm_g, w, qg, kg, cos, sin, bd, dtb)


def _od_attn_kernel(sink_ref, q_ref, k_ref, v_ref, o_ref, *, n_ctx, t, nctx_tiles):
    g = pl.program_id(1)
    i = pl.program_id(2)
    tm = q_ref.shape[0]
    span = tm + 2 * C_WINDOW
    grp = C_HEADS // C_KV_HEADS
    start = pl.multiple_of(jnp.clip(i * tm - C_WINDOW, n_ctx, t - span), LANES)
    kl = k_ref[pl.ds(start, span), :]
    vl = v_ref[pl.ds(start, span), :]
    kc = k_ref[0:n_ctx, :]
    vc = v_ref[0:n_ctx, :]
    qpos = i * tm + lax.broadcasted_iota(jnp.int32, (tm, span), 0)
    kpos = start + lax.broadcasted_iota(jnp.int32, (tm, span), 1)
    band = jnp.logical_and(jnp.abs(qpos - kpos) <= C_WINDOW, i >= nctx_tiles)
    low = lax.broadcasted_iota(jnp.int32, (1, LANES), 1) < C_HEAD_DIM
    zero = jnp.zeros((), BF16)
    kl_par = (jnp.where(low, kl, zero), jnp.where(low, zero, kl))
    kc_par = (jnp.where(low, kc, zero), jnp.where(low, zero, kc))
    for j in range(grp // 2):
        q2 = q_ref[:, j * LANES:(j + 1) * LANES]
        outs = []
        for par in range(2):
            sk = sink_ref[g * grp + 2 * j + par] * LOG2E
            s_l = jnp.where(band, _dot_nt(q2, kl_par[par]), NEG)
            s_c = _dot_nt(q2, kc_par[par])
            m = jnp.maximum(jnp.maximum(jnp.max(s_l, axis=-1, keepdims=True),
                                        jnp.max(s_c, axis=-1, keepdims=True)), sk)
            p_l = jnp.exp2(s_l - m)
            p_c = jnp.exp2(s_c - m)
            den = jnp.sum(p_l, axis=-1, keepdims=True) + jnp.sum(p_c, axis=-1, keepdims=True) + jnp.exp2(sk - m)
            outs.append((_dot(p_l.astype(BF16), vl) + _dot(p_c.astype(BF16), vc)) / den)
        o_ref[:, j * LANES:(j + 1) * LANES] = jnp.where(low, outs[0], outs[1]).astype(BF16)


def _od_attn(sink, q, k, v, n_ctx):
    b_sz, t, _ = q.shape
    grp_w = (C_HEADS // C_KV_HEADS) * C_HEAD_DIM
    kern = functools.partial(_od_attn_kernel, n_ctx=n_ctx, t=t, nctx_tiles=n_ctx // TM)
    kv_spec = pl.BlockSpec((None, t, LANES), lambda b, g, i: (b, 0, g))
    return pl.pallas_call(
        kern,
        grid=(b_sz, C_KV_HEADS, t // TM),
        in_specs=[pl.BlockSpec(memory_space=pltpu.SMEM),
                  pl.BlockSpec((None, TM, grp_w), lambda b, g, i: (b, i, g)), kv_spec, kv_spec],
        out_specs=pl.BlockSpec((None, TM, grp_w), lambda b, g, i: (b, i, g)),
        out_shape=jax.ShapeDtypeStruct((b_sz, t, C_Q), BF16),
        compiler_params=_cparams(("parallel", "parallel", "arbitrary")),
        name="odd_attention",
    )(sink, q, k, v)


def _od_conv_kernel(u_ref, up_ref, un_ref, w_ref, b_ref, xs_ref, bc_ref, buf, *, nctx_tiles, n_tiles, rb):
    i = pl.program_id(1)
    has_prev, has_next = _halo_flags(i, nctx_tiles, n_tiles)
    tm = u_ref.shape[0]
    buf[0:OD_HALO, :] = jnp.where(has_prev, up_ref[...], 0.0)
    buf[OD_HALO:OD_HALO + tm, :] = u_ref[...]
    buf[OD_HALO + tm:, :] = jnp.where(has_next, un_ref[...], 0.0)
    base = OD_HALO - D_CONV // 2
    for r in range(tm // rb):
        acc = jnp.zeros((rb, D_XBC), F32) + b_ref[...]
        for k in range(D_CONV):
            acc = acc + buf[pl.ds(r * rb + base + k, rb), :] * w_ref[k:k + 1, :]
        y = _silu(acc)
        xs_ref[r * rb:(r + 1) * rb, :] = y[:, 0:D_INNER]
        bc_ref[r * rb:(r + 1) * rb, :] = y[:, D_INNER:].astype(BF16)


def _od_conv(xbc, w, bias, n_ctx):
    b_sz, t, c = xbc.shape
    n_tiles = t // TM
    prev, nxt = _halo_specs(t, c, OD_HALO, n_tiles)
    kern = functools.partial(_od_conv_kernel, nctx_tiles=n_ctx // TM, n_tiles=n_tiles, rb=16)
    out = lambda n: pl.BlockSpec((None, TM, n), lambda b, i: (b, i, 0))
    return pl.pallas_call(
        kern,
        grid=(b_sz, n_tiles),
        in_specs=[pl.BlockSpec((None, TM, c), lambda b, i: (b, i, 0)), prev, nxt, _full(w.shape), _full(bias.shape)],
        out_specs=[out(D_INNER), out(c - D_INNER)],
        out_shape=[jax.ShapeDtypeStruct((b_sz, t, D_INNER), F32),
                   jax.ShapeDtypeStruct((b_sz, t, c - D_INNER), BF16)],
        scratch_shapes=[pltpu.VMEM((TM + 2 * OD_HALO, c), F32)],
        compiler_params=_cparams(("parallel", "parallel")),
        name="odd_ssd_conv",
    )(xbc, xbc, xbc, w, bias)


def _ssd_kernel(*refs, reverse, gated):
    if gated:
        xs_ref, bc_ref, dt_ref, av_ref, ex_ref, dsk_ref, y1_ref, z_ref, gn_ref, o_ref, h_ref = refs
    else:
        xs_ref, bc_ref, dt_ref, av_ref, ex_ref, dsk_ref, o_ref, h_ref = refs
    q = D_CHUNK
    gw = D_INNER // D_GROUPS
    hpg = D_HEADS // D_GROUPS
    col0 = D_HEADS if reverse else 0

    @pl.when(pl.program_id(1) == 0)
    def _():
        h_ref[...] = jnp.zeros_like(h_ref)

    dt = dt_ref[...]
    a = dt * av_ref[...]
    ii = lax.broadcasted_iota(jnp.int32, (q, q), 0)
    jj = lax.broadcasted_iota(jnp.int32, (q, q), 1)
    tri = (jj >= ii) if reverse else (jj <= ii)
    a1 = a.astype(BF16)
    r1 = a - a1.astype(F32)
    a2 = r1.astype(BF16)
    a3 = (r1 - a2.astype(F32)).astype(BF16)
    cs = _dot(jnp.where(tri, 1.0, 0.0).astype(BF16), jnp.concatenate([a1, a2, a3], axis=1))
    acs = cs[:, 0:LANES] + cs[:, LANES:2 * LANES] + cs[:, 2 * LANES:3 * LANES]
    edge = 0 if reverse else q - 1
    tot = acs[edge:edge + 1, :]

    def expand(w):
        hi, lo = _split2(w)
        return _dot(jnp.concatenate([hi, lo], axis=1), ex_ref[...])

    xs = xs_ref[...]
    w_y = expand(jnp.exp(acs))
    xdt = (xs * expand(dt)).astype(BF16)
    xw = (xs * expand(dt * jnp.exp(tot - acs))).astype(BF16)
    acs_t = acs.T
    bc = bc_ref[...]
    h_in = h_ref[...]
    hb = h_in.astype(BF16)
    low = lax.broadcasted_iota(jnp.int32, (1, LANES), 1) < D_HEAD_DIM
    ys = []
    for g in range(D_GROUPS):
        gs = slice(g * gw, (g + 1) * gw)
        b_g = bc[:, g * D_STATE:(g + 1) * D_STATE]
        c_g = bc[:, (D_GROUPS + g) * D_STATE:(D_GROUPS + g + 1) * D_STATE]
        cb = _dot_nt(c_g, b_g)
        inter = _dot(c_g, hb[:, gs]) * w_y[:, gs]
        for pr in range(hpg // 2):
            x2 = xdt[:, g * gw + pr * LANES:g * gw + (pr + 1) * LANES]
            outs = []
            for par in range(2):
                c = col0 + g * hpg + 2 * pr + par
                seg = acs[:, c:c + 1] - acs_t[c:c + 1, :]
                lmat = (jnp.exp(jnp.where(tri, seg, NEG)) * cb).astype(BF16)
                outs.append(_dot(lmat, x2))
            ys.append(jnp.where(low, outs[0], outs[1]) + inter[:, pr * LANES:(pr + 1) * LANES])
        h_ref[:, gs] = h_in[:, gs] * w_y[edge:edge + 1, gs] + _dot_tn(b_g, xw[:, gs])
    y = jnp.concatenate(ys, axis=1)
    if gated:
        gz = (y1_ref[...] + y) * _silu(z_ref[...])
        ms = jnp.mean(gz * gz, axis=-1, keepdims=True)
        o_ref[...] = (gz * lax.rsqrt(ms + EPS) * gn_ref[...]).astype(BF16)
    else:
        o_ref[...] = dsk_ref[...] * xs + y


def _ssd(xs, bc, dt, avec, ex, dsk, n_ctx, reverse, y1=None, z=None, gn=None):
    b_sz, t, _ = xs.shape
    q = D_CHUNK
    nc = t // q
    ncc = n_ctx // q
    if reverse:
        cmap = lambda b, s: (b, jnp.where(s < ncc, ncc - 1 - s, nc - 1 - (s - ncc)), 0)
    else:
        cmap = lambda b, s: (b, s, 0)
    row = lambda n: pl.BlockSpec((None, q, n), cmap)
    gated = y1 is not None
    args = [xs, bc, dt, avec, ex, dsk]
    specs = [row(D_INNER), row(bc.shape[-1]), row(LANES), _full(avec.shape), _full(ex.shape), _full(dsk.shape)]
    if gated:
        args += [y1, z, gn]
        specs += [row(D_INNER), row(D_INNER), _full(gn.shape)]
    return pl.pallas_call(
        functools.partial(_ssd_kernel, reverse=reverse, gated=gated),
        grid=(b_sz, nc),
        in_specs=specs,
        out_specs=row(D_INNER),
        out_shape=jax.ShapeDtypeStruct((b_sz, t, D_INNER), BF16 if gated else F32),
        scratch_shapes=[pltpu.VMEM((D_STATE, D_INNER), F32)],
        compiler_params=_cparams(("parallel", "arbitrary")),
        name="ssd_scan_bwd_gate" if gated else "ssd_scan_fwd",
    )(*args)


def _rope_tables(seq, n_ctx, head_dim):
    pos = jnp.arange(seq)
    row = (pos // GRID_W).astype(F32)
    col = (pos % GRID_W).astype(F32)
    quarter = head_dim // 4
    inv = ROPE_THETA ** (-jnp.arange(quarter, dtype=F32) / quarter)
    ar = row[:, None] * inv
    ac = col[:, None] * inv
    ang = jnp.concatenate([ar, ar, ac, ac], axis=-1)
    sign = jnp.where((jnp.arange(head_dim) % (head_dim // 2)) < quarter, -1.0, 1.0).astype(F32)
    cos = jnp.concatenate([jnp.ones((n_ctx, head_dim), F32), jnp.cos(ang)], axis=0)
    sin = jnp.concatenate([jnp.zeros((n_ctx, head_dim), F32), jnp.sin(ang) * sign], axis=0)
    rep = LANES // head_dim
    return jnp.tile(cos, (1, rep)), jnp.tile(sin, (1, rep))


def _dup_heads(w, head_dim):
    d, n = w.shape
    w = w.reshape(d, n // head_dim, 1, head_dim)
    return jnp.broadcast_to(w, (d, n // head_dim, 2, head_dim)).reshape(d, 2 * n)


def kernel(x, c, ctx, c_ctx, mod_w, mod_b, norm1_g, norm2_g, ffn_w_in, ffn_w_out, ev_w_in, ev_w_out, ev_q_g, ev_k_g, ev_dw_w, ev_dw_b, ev_cn_g, ev_cn_b, od_w_in, od_w_out, od_q_g, od_k_g, od_sink, od_conv_w, od_conv_b, od_dt_bias, od_A_log, od_D, od_gnorm_g):
    b_sz, seq, d = x.shape
    n_ctx = ctx.shape[1]
    depth = mod_w.shape[0]
    assert seq % TM == 0 and n_ctx % TM == 0 and seq >= TM + 2 * C_WINDOW and b_sz + 1 <= MOD_ROWS
    nctx_tiles = n_ctx // TM

    xs = jnp.concatenate([ctx, x], axis=1)
    cc = jnp.zeros((MOD_ROWS, d), F32).at[:b_sz].set(c).at[b_sz].set(c_ctx)
    mod = _modulation(cc, mod_w, mod_b).reshape(depth, MOD_ROWS, 6, d)
    g1 = norm1_g.reshape(depth, 1, d)
    g2 = norm2_g.reshape(depth, 1, d)

    cos_a, sin_a = _rope_tables(seq, n_ctx, A_HEAD_DIM)
    cos_c, sin_c = _rope_tables(seq, n_ctx, C_HEAD_DIM)
    lane = jnp.arange(LANES)
    bd = (lane[:, None] // C_HEAD_DIM == lane[None, :] // C_HEAD_DIM).astype(BF16)
    bd = jnp.concatenate([bd, bd], axis=0)
    head_of_lane = jnp.arange(D_INNER) // D_HEAD_DIM
    ex = []
    for dirn in range(2):
        e = (lane[:, None] == head_of_lane[None, :] + dirn * D_HEADS).astype(BF16)
        ex.append(jnp.concatenate([e, e], axis=0))

    for l in range(depth):
        i = l // 2
        wi = ffn_w_in[l].astype(BF16)
        wo = ffn_w_out[l].astype(BF16)
        if l % 2 == 0:
            w_in = ev_w_in[i].astype(BF16)
            w_out = ev_w_out[i].astype(BF16)
            q, k, v, u = _ev_in(xs, mod, g1, w_in, ev_q_g[i].reshape(1, -1), ev_k_g[i].reshape(1, -1),
                                cos_a, sin_a, l, nctx_tiles)
            a1 = _ev_attn(q, k, v, n_ctx)
            a2 = _ev_conv(u, ev_dw_w[i], ev_dw_b[i].reshape(1, -1), ev_cn_g[i].reshape(1, -1),
                          ev_cn_b[i].reshape(1, -1), n_ctx)
            wo1, wo2 = w_out[:A_Q], w_out[A_Q:]
        else:
            w = od_w_in[i]
            o_k, o_v, o_x, o_dt, o_z = C_Q, C_Q + C_KV, C_Q + 2 * C_KV, C_Q + 2 * C_KV + D_XBC, C_Q + 2 * C_KV + D_XBC + 2 * D_HEADS
            w_in = jnp.concatenate([
                w[:, :o_k], _dup_heads(w[:, o_k:o_v], C_HEAD_DIM), _dup_heads(w[:, o_v:o_x], C_HEAD_DIM),
                w[:, o_x:o_dt], w[:, o_z:], w[:, o_dt:o_z], jnp.zeros((d, LANES - 2 * D_HEADS), F32)], axis=1).astype(BF16)
            w_out = od_w_out[i].astype(BF16)
            pad = jnp.zeros((LANES - 2 * D_HEADS,), F32)
            dtb = jnp.concatenate([od_dt_bias[i].reshape(-1), pad]).reshape(1, LANES)
            avec = jnp.concatenate([-jnp.exp(od_A_log[i].astype(F32)).reshape(-1), pad]).reshape(1, LANES)
            dsk = jnp.repeat(od_D[i].astype(F32), D_HEAD_DIM).reshape(1, D_INNER)
            qg = jnp.tile(od_q_g[i], LANES // C_HEAD_DIM).reshape(1, LANES)
            kg = jnp.tile(od_k_g[i], LANES // C_HEAD_DIM).reshape(1, LANES)
            q, k, v, xbc, z, dt = _od_in(xs, mod, g1, w_in, qg, kg, cos_c, sin_c, bd, dtb, l, nctx_tiles)
            a1 = _od_attn(od_sink[i], q, k, v, n_ctx)
            xc, bc = _od_conv(xbc, od_conv_w[i], od_conv_b[i].reshape(1, -1), n_ctx)
            y1 = _ssd(xc, bc, dt, avec, ex[0], dsk, n_ctx, reverse=False)
            a2 = _ssd(xc, bc, dt, avec, ex[1], dsk, n_ctx, reverse=True, y1=y1, z=z,
                      gn=od_gnorm_g[i].reshape(1, -1))
            wo1, wo2 = w_out[:C_Q], w_out[C_Q:]
        xs = _out_ffn(xs, a1, a2, mod, g2, wo1, wo2, wi, wo, l, nctx_tiles)
    return xs[:, n_ctx:]
```

```python
import functools
import math

import jax
import jax.numpy as jnp
from jax import lax
from jax.experimental import pallas as pl
from jax.experimental.pallas import tpu as pltpu

F32 = jnp.float32
BF16 = jnp.bfloat16

EPS = 1e-6
ROPE_THETA = 10000.0
GRID_W = 64
LOG2E = math.log2(math.e)
NEG = -1e30

A_HEADS, A_KV_HEADS, A_HEAD_DIM = 8, 2, 128
B_WIDTH, B_CONV = 512, 31
C_HEADS, C_KV_HEADS, C_HEAD_DIM, C_WINDOW = 16, 2, 64, 128
D_INNER, D_HEAD_DIM, D_HEADS, D_STATE, D_GROUPS, D_CONV, D_CHUNK = 1024, 64, 16, 128, 2, 5, 128
A_Q, A_KV = A_HEADS * A_HEAD_DIM, A_KV_HEADS * A_HEAD_DIM
C_Q, C_KV = C_HEADS * C_HEAD_DIM, C_KV_HEADS * C_HEAD_DIM
D_XBC = D_INNER + 2 * D_GROUPS * D_STATE

LANES = 128
SUBLANES = 8
VMEM_BYTES = 64 * 1024 * 1024

TM = 256
MOD_ROWS = 16
EV_HALO = 16
OD_HALO = 8


def _cparams(sem, vmem_mb=None):
    kw = dict(dimension_semantics=sem)
    if vmem_mb is not None:
        kw["vmem_limit_bytes"] = vmem_mb * 1024 * 1024
    return pltpu.CompilerParams(**kw)


def _dot(a, b):
    return jnp.dot(a, b, preferred_element_type=F32)


def _dot_nt(a, b):
    return lax.dot_general(a, b, (((1,), (1,)), ((), ())), preferred_element_type=F32)


def _dot_tn(a, b):
    return lax.dot_general(a, b, (((0,), (0,)), ((), ())), preferred_element_type=F32)


def _silu(x):
    return x * jax.nn.sigmoid(x)


def _split2(x):
    hi = x.astype(BF16)
    lo = (x - hi.astype(F32)).astype(BF16)
    return hi, lo


def _prenorm(x, g, shift, scale):
    ms = jnp.mean(x * x, axis=-1, keepdims=True)
    return (x * lax.rsqrt(ms + EPS) * g) * (1.0 + scale) + shift


def _mod_kernel(c_ref, w_ref, b_ref, o_ref):
    a = _silu(c_ref[...]).astype(BF16)
    o_ref[...] = _dot(a, w_ref[...].astype(BF16)) + b_ref[...]


def _modulation(cc, mod_w, mod_b):
    depth, d, n = mod_w.shape
    tn = 6 * LANES * 2
    return pl.pallas_call(
        _mod_kernel,
        grid=(depth, n // tn),
        in_specs=[pl.BlockSpec((MOD_ROWS, d), lambda l, j: (0, 0)),
                  pl.BlockSpec((None, d, tn), lambda l, j: (l, 0, j)),
                  pl.BlockSpec((None, 1, tn), lambda l, j: (l, 0, j))],
        out_specs=pl.BlockSpec((None, MOD_ROWS, tn), lambda l, j: (l, 0, j)),
        out_shape=jax.ShapeDtypeStruct((depth, MOD_ROWS, n), F32),
        compiler_params=_cparams(("parallel", "parallel")),
        name="modulation",
    )(cc, mod_w, mod_b.reshape(depth, 1, n))


def _ev_in_kernel(*refs, n_src, nctx_tiles):
    mod_ref, g_ref, w_ref, qg_ref, kg_ref, cos_ref, sin_ref, q_ref, k_ref, vt_ref, u_ref = refs[n_src:]
    x = _load_stream(refs[:n_src], nctx_tiles)
    hb = _prenorm(x, g_ref[...], mod_ref[0:1, :], mod_ref[1:2, :]).astype(BF16)
    cos = cos_ref[...]
    sin = sin_ref[...]
    lane = lax.broadcasted_iota(jnp.int32, (1, LANES), 1)
    first = (lane % (A_HEAD_DIM // 2)) < (A_HEAD_DIM // 4)

    def norm_rope(t, g):
        y = t * lax.rsqrt(jnp.mean(t * t, axis=-1, keepdims=True) + EPS) * g
        rot = jnp.where(first, pltpu.roll(y, LANES - A_HEAD_DIM // 4, 1), pltpu.roll(y, A_HEAD_DIM // 4, 1))
        return y * cos + rot * sin

    qkv = _dot(hb, w_ref[:, 0:A_Q + 2 * A_KV])
    qg = qg_ref[...] * (A_HEAD_DIM ** -0.5 * LOG2E)
    for h in range(A_HEADS):
        sl = slice(h * LANES, (h + 1) * LANES)
        q_ref[:, sl] = norm_rope(qkv[:, sl], qg).astype(BF16)
    for h in range(A_KV_HEADS):
        sl = slice(h * LANES, (h + 1) * LANES)
        k_ref[:, sl] = norm_rope(qkv[:, A_Q + h * LANES:A_Q + (h + 1) * LANES], kg_ref[...]).astype(BF16)
    for h in range(A_KV_HEADS):
        vt_ref[h] = qkv[:, A_Q + A_KV + h * LANES:A_Q + A_KV + (h + 1) * LANES].T.astype(BF16)
    glu = _dot(hb, w_ref[:, A_Q + 2 * A_KV:])
    u_ref[...] = glu[:, :B_WIDTH] * jax.nn.sigmoid(glu[:, B_WIDTH:])


def _row_specs(b_sz, nctx_tiles, d, layer, first_tile=0):
    mod_spec = pl.BlockSpec((None, None, 6, d),
                            lambda b, i: (layer, jnp.where(i + first_tile < nctx_tiles, b_sz, b), 0, 0))
    g_spec = pl.BlockSpec((None, 1, d), lambda b, i: (layer, 0, 0))
    return mod_spec, g_spec


def _stream_specs(src, nctx_tiles, first_tile=0):
    if not isinstance(src, tuple):
        d = src.shape[-1]
        return [pl.BlockSpec((None, TM, d), lambda b, i: (b, i + first_tile, 0))], [src], src.shape
    ctx, x = src
    d = x.shape[-1]
    c_spec = pl.BlockSpec((None, TM, d), lambda b, i: (b, jnp.minimum(i + first_tile, nctx_tiles - 1), 0))
    x_spec = pl.BlockSpec((None, TM, d), lambda b, i: (b, jnp.maximum(i + first_tile - nctx_tiles, 0), 0))
    return [c_spec, x_spec], [ctx, x], (x.shape[0], ctx.shape[1] + x.shape[1], d)


def _load_stream(src_refs, nctx_tiles, first_tile=0):
    if len(src_refs) == 1:
        return src_refs[0][...]
    return jnp.where(pl.program_id(1) + first_tile < nctx_tiles, src_refs[0][...], src_refs[1][...])


def _full(shape):
    nd = len(shape)
    return pl.BlockSpec(shape, lambda *_: (0,) * nd, pipeline_mode=pl.Buffered(1))


def _ev_in(src, mod, norm_g, w, qg, kg, cos, sin, layer, nctx_tiles):
    src_specs, src_arrays, (b_sz, t, d) = _stream_specs(src, nctx_tiles)
    mod_spec, g_spec = _row_specs(b_sz, nctx_tiles, d, layer)
    tab = pl.BlockSpec((TM, LANES), lambda b, i: (i, 0))
    out = lambda n: pl.BlockSpec((None, TM, n), lambda b, i: (b, i, 0))
    return pl.pallas_call(
        functools.partial(_ev_in_kernel, n_src=len(src_arrays), nctx_tiles=nctx_tiles),
        grid=(b_sz, t // TM),
        in_specs=src_specs + [mod_spec, g_spec, _full(w.shape), _full(qg.shape), _full(kg.shape), tab, tab],
        out_specs=[out(A_Q), out(A_KV),
                   pl.BlockSpec((None, A_KV_HEADS, None, A_HEAD_DIM, TM), lambda b, i: (b, 0, i, 0, 0)),
                   out(B_WIDTH)],
        out_shape=[jax.ShapeDtypeStruct((b_sz, t, A_Q), BF16),
                   jax.ShapeDtypeStruct((b_sz, t, A_KV), BF16),
                   jax.ShapeDtypeStruct((b_sz, A_KV_HEADS, t // TM, A_HEAD_DIM, TM), BF16),
                   jax.ShapeDtypeStruct((b_sz, t, B_WIDTH), F32)],
        compiler_params=_cparams(("parallel", "parallel"), 48),
        name="even_in_proj",
    )(*src_arrays, mod, norm_g, w, qg, kg, cos, sin)


def _ev_attn_kernel(q_ref, k_ref, vt_ref, o_ref, acc_ref, s_ref, *, kb_per_step, n_ctx, nctx_tiles):
    i = pl.program_id(2)
    tm = q_ref.shape[0]
    kb = vt_ref.shape[-1]
    grp = A_HEADS // A_KV_HEADS
    m_rows = grp * tm
    q = jnp.concatenate([q_ref[:, j * LANES:(j + 1) * LANES] for j in range(grp)], axis=0)

    def softmax_pv(s, vt, m, l, first):
        m_new = jnp.maximum(m, jnp.max(s, axis=0, keepdims=True))
        p = jnp.exp2(s - m_new)
        alpha = jnp.exp2(m - m_new)
        l_new = alpha * l + jnp.sum(p, axis=0, keepdims=True)
        pv = _dot(vt, p.astype(BF16))
        acc_ref[...] = pv if first else acc_ref[...] * alpha + pv
        return m_new, l_new

    def finish(l):
        o = (acc_ref[...] / l).T
        for j in range(grp):
            o_ref[:, j * LANES:(j + 1) * LANES] = o[j * tm:(j + 1) * tm].astype(BF16)

    nb_ctx = n_ctx // kb

    def ctx_block(s_ctx):
        vt = jnp.concatenate([vt_ref[j] for j in range(nb_ctx)], axis=1)
        return softmax_pv(s_ctx, vt, jnp.full((1, m_rows), NEG, F32), jnp.zeros((1, m_rows), F32), True)

    @pl.when(i < nctx_tiles)
    def _():
        _, l_ctx = ctx_block(_dot_nt(k_ref[0:n_ctx, :], q))
        finish(l_ctx)

    kps = kb_per_step

    def scores(blk, slot):
        off = pl.multiple_of(blk * kb, kb)
        s_ref[slot] = _dot_nt(k_ref[pl.ds(off, kps * kb), :], q)

    def consume(blk, slot, carry):
        vt = jnp.concatenate([vt_ref[blk + r] for r in range(kps)], axis=1)
        return softmax_pv(s_ref[slot], vt, *carry, False)

    n_pairs = (vt_ref.shape[0] - nb_ctx) // (2 * kps)

    @pl.when(i >= nctx_tiles)
    def _():
        s_ctx = _dot_nt(k_ref[0:n_ctx, :], q)
        scores(nb_ctx, 0)
        m, l = ctx_block(s_ctx)

        def body(jj, carry):
            blk = nb_ctx + 2 * kps * jj
            scores(blk + kps, 1)
            carry = consume(blk, 0, carry)
            scores(blk + 2 * kps, 0)
            return consume(blk + kps, 1, carry)

        carry = lax.fori_loop(0, n_pairs - 1, body, (m, l))
        blk = nb_ctx + 2 * kps * (n_pairs - 1)
        scores(blk + kps, 1)
        carry = consume(blk, 0, carry)
        _, l_fin = consume(blk + kps, 1, carry)
        finish(l_fin)


def _ev_attn(q, k, vt, n_ctx):
    b_sz, t, _ = q.shape
    nblk, kb = vt.shape[2], vt.shape[4]
    kb_per_step = 4
    assert (nblk - n_ctx // kb) % (2 * kb_per_step) == 0
    grp = A_HEADS // A_KV_HEADS
    grp_w = grp * A_HEAD_DIM
    kern = functools.partial(_ev_attn_kernel, kb_per_step=kb_per_step, n_ctx=n_ctx, nctx_tiles=n_ctx // TM)
    return pl.pallas_call(
        kern,
        grid=(b_sz, A_KV_HEADS, t // TM),
        in_specs=[pl.BlockSpec((None, TM, grp_w), lambda b, g, i: (b, i, g)),
                  pl.BlockSpec((None, t, A_HEAD_DIM), lambda b, g, i: (b, 0, g)),
                  pl.BlockSpec((None, None, nblk, A_HEAD_DIM, kb), lambda b, g, i: (b, g, 0, 0, 0))],
        out_specs=pl.BlockSpec((None, TM, grp_w), lambda b, g, i: (b, i, g)),
        out_shape=jax.ShapeDtypeStruct((b_sz, t, A_Q), BF16),
        scratch_shapes=[pltpu.VMEM((A_HEAD_DIM, grp * TM), F32),
                        pltpu.VMEM((2, kb_per_step * kb, grp * TM), F32)],
        compiler_params=_cparams(("parallel", "parallel", "arbitrary"), 48),
        name="even_attention",
    )(q, k, vt)


def _halo_flags(i, nctx_tiles, n_tiles):
    has_prev = jnp.logical_and(i != 0, i != nctx_tiles)
    has_next = jnp.logical_and(i != nctx_tiles - 1, i != n_tiles - 1)
    return has_prev, has_next


def _ev_conv_kernel(u_ref, up_ref, un_ref, w_ref, b_ref, g_ref, bb_ref, o_ref, buf, sh, *, nctx_tiles, n_tiles, rb):
    i = pl.program_id(1)
    has_prev, has_next = _halo_flags(i, nctx_tiles, n_tiles)
    tm = u_ref.shape[0]
    buf[0:EV_HALO, :] = jnp.where(has_prev, up_ref[...], 0.0)
    buf[EV_HALO:EV_HALO + tm, :] = u_ref[...]
    buf[EV_HALO + tm:, :] = jnp.where(has_next, un_ref[...], 0.0)
    for s in range(1, SUBLANES):
        sh[s - 1] = buf[pl.ds(s, sh.shape[1]), :]
    base = EV_HALO - B_CONV // 2
    for r in range(tm // rb):
        acc = jnp.zeros((rb, B_WIDTH), F32) + b_ref[...]
        for k in range(B_CONV):
            s = (base + k) % SUBLANES
            al = r * rb + base + k - s
            src = buf[al:al + rb, :] if s == 0 else sh[s - 1, al:al + rb, :]
            acc = acc + src * w_ref[k:k + 1, :]
        mu = jnp.mean(acc, axis=-1, keepdims=True)
        cen = acc - mu
        var = jnp.mean(cen * cen, axis=-1, keepdims=True)
        y = cen * lax.rsqrt(var + EPS) * g_ref[...] + bb_ref[...]
        o_ref[r * rb:(r + 1) * rb, :] = _silu(y).astype(BF16)


def _halo_specs(t, width, halo):
    per = TM // halo
    last = t // halo - 1
    prev = pl.BlockSpec((None, halo, width), lambda b, i: (b, jnp.maximum(i * per - 1, 0), 0))
    nxt = pl.BlockSpec((None, halo, width), lambda b, i: (b, jnp.minimum((i + 1) * per, last), 0))
    return prev, nxt


def _ev_conv(u, w, bias, g, bb, n_ctx):
    b_sz, t, c = u.shape
    n_tiles = t // TM
    prev, nxt = _halo_specs(t, c, EV_HALO)
    kern = functools.partial(_ev_conv_kernel, nctx_tiles=n_ctx // TM, n_tiles=n_tiles, rb=32)
    return pl.pallas_call(
        kern,
        grid=(b_sz, n_tiles),
        in_specs=[pl.BlockSpec((None, TM, c), lambda b, i: (b, i, 0)), prev, nxt,
                  _full(w.shape), _full(bias.shape), _full(g.shape), _full(bb.shape)],
        out_specs=pl.BlockSpec((None, TM, c), lambda b, i: (b, i, 0)),
        out_shape=jax.ShapeDtypeStruct((b_sz, t, c), BF16),
        scratch_shapes=[pltpu.VMEM((TM + 2 * EV_HALO, c), F32),
                        pltpu.VMEM((SUBLANES - 1, TM + 2 * EV_HALO - SUBLANES, c), F32)],
        compiler_params=_cparams(("parallel", "parallel")),
        name="even_conv_module",
    )(u, u, u, w, bias, g, bb)


def _out_ffn_kernel(*refs, n_src, nctx_tiles, first_tile, hid, chunk):
    a1_ref, a2_ref, mod_ref, g_ref, wo1_ref, wo2_ref, wi_ref, wo_ref, o_ref, act_ref = refs[n_src:]
    o = _dot(a1_ref[...], wo1_ref[...]) + _dot(a2_ref[...], wo2_ref[...])
    x1 = _load_stream(refs[:n_src], nctx_tiles, first_tile) + mod_ref[2:3, :] * o
    h2 = _prenorm(x1, g_ref[...], mod_ref[3:4, :], mod_ref[4:5, :]).astype(BF16)
    for c in range(hid // chunk):
        gte = _dot(h2, wi_ref[:, c * chunk:(c + 1) * chunk])
        up = _dot(h2, wi_ref[:, hid + c * chunk:hid + (c + 1) * chunk])
        act_ref[:, c * chunk:(c + 1) * chunk] = (_silu(gte) * up).astype(BF16)
    o_ref[...] = x1 + mod_ref[5:6, :] * _dot(act_ref[...], wo_ref[...])


def _out_ffn(src, a1, a2, mod, norm_g, wo1, wo2, wi, wo, layer, nctx_tiles, latent_only):
    first = nctx_tiles if latent_only else 0
    src_specs, src_arrays, (b_sz, t, d) = _stream_specs(src, nctx_tiles, first)
    hid = wo.shape[0]
    n_tiles = t // TM - first
    mod_spec, g_spec = _row_specs(b_sz, nctx_tiles, d, layer, first)
    row = lambda n: pl.BlockSpec((None, TM, n), lambda b, i: (b, i + first, 0))
    kern = functools.partial(_out_ffn_kernel, n_src=len(src_arrays), nctx_tiles=nctx_tiles, first_tile=first,
                             hid=hid, chunk=2 * LANES)
    in_place = len(src_arrays) == 1 and not latent_only
    return pl.pallas_call(
        kern,
        grid=(b_sz, n_tiles),
        in_specs=src_specs + [row(a1.shape[-1]), row(a2.shape[-1]), mod_spec, g_spec,
                              _full(wo1.shape), _full(wo2.shape), _full(wi.shape), _full(wo.shape)],
        out_specs=pl.BlockSpec((None, TM, d), lambda b, i: (b, i, 0)),
        out_shape=jax.ShapeDtypeStruct((b_sz, n_tiles * TM, d), F32),
        scratch_shapes=[pltpu.VMEM((TM, hid), BF16)],
        input_output_aliases={0: 0} if in_place else {},
        compiler_params=_cparams(("parallel", "parallel"), 56),
        name="out_proj_ffn",
    )(*src_arrays, a1, a2, mod, norm_g, wo1, wo2, wi, wo)


def _od_in_kernel(x_ref, mod_ref, g_ref, w_ref, qg_ref, kg_ref, cos_ref, sin_ref, bd_ref, dtb_ref,
                  q_ref, k_ref, vt_ref, xbc_ref, z_ref, dt_ref):
    hb = _prenorm(x_ref[...], g_ref[...], mod_ref[0:1, :], mod_ref[1:2, :]).astype(BF16)
    cos = cos_ref[...]
    sin = sin_ref[...]
    bd = bd_ref[...]
    lane = lax.broadcasted_iota(jnp.int32, (1, LANES), 1)
    first = (lane % (C_HEAD_DIM // 2)) < (C_HEAD_DIM // 4)

    def norm_rope(t, g):
        hi, lo = _split2(t * t)
        ss = _dot(jnp.concatenate([hi, lo], axis=1), bd)
        y = t * lax.rsqrt(ss * (1.0 / C_HEAD_DIM) + EPS) * g
        rot = jnp.where(first, pltpu.roll(y, LANES - C_HEAD_DIM // 4, 1), pltpu.roll(y, C_HEAD_DIM // 4, 1))
        return y * cos + rot * sin

    nq = C_Q // LANES
    nk = 2 * C_KV // LANES
    qkv = _dot(hb, w_ref[:, 0:C_Q + 4 * C_KV])
    qg = qg_ref[...] * (C_HEAD_DIM ** -0.5 * LOG2E)
    for j in range(nq):
        sl = slice(j * LANES, (j + 1) * LANES)
        q_ref[:, sl] = norm_rope(qkv[:, sl], qg).astype(BF16)
    for j in range(nk):
        k_ref[:, j * LANES:(j + 1) * LANES] = norm_rope(qkv[:, C_Q + j * LANES:C_Q + (j + 1) * LANES], kg_ref[...]).astype(BF16)
    for j in range(nk):
        vt_ref[j] = qkv[:, C_Q + 2 * C_KV + j * LANES:C_Q + 2 * C_KV + (j + 1) * LANES].T.astype(BF16)
    rest = _dot(hb, w_ref[:, C_Q + 4 * C_KV:])
    xbc_ref[...] = rest[:, 0:D_XBC]
    z_ref[...] = rest[:, D_XBC:D_XBC + D_INNER]
    dtr = rest[:, D_XBC + D_INNER:] + dtb_ref[...]
    dt_ref[...] = jnp.maximum(dtr, 0.0) + jnp.log1p(jnp.exp(-jnp.abs(dtr)))


def _od_in(xs, mod, norm_g, w, qg, kg, cos, sin, bd, dtb, layer, nctx_tiles):
    b_sz, t, d = xs.shape
    x_spec = pl.BlockSpec((None, TM, d), lambda b, i: (b, i, 0))
    mod_spec, g_spec = _row_specs(b_sz, nctx_tiles, d, layer)
    tab = pl.BlockSpec((TM, LANES), lambda b, i: (i, 0))
    out = lambda n: pl.BlockSpec((None, TM, n), lambda b, i: (b, i, 0))
    return pl.pallas_call(
        _od_in_kernel,
        grid=(b_sz, t // TM),
        in_specs=[x_spec, mod_spec, g_spec, _full(w.shape), _full(qg.shape), _full(kg.shape), tab, tab,
                  _full(bd.shape), _full(dtb.shape)],
        out_specs=[out(C_Q), out(2 * C_KV),
                   pl.BlockSpec((None, C_KV_HEADS, None, LANES, TM), lambda b, i: (b, 0, i, 0, 0)),
                   out(D_XBC), out(D_INNER), out(LANES)],
        out_shape=[jax.ShapeDtypeStruct((b_sz, t, C_Q), BF16),
                   jax.ShapeDtypeStruct((b_sz, t, 2 * C_KV), BF16),
                   jax.ShapeDtypeStruct((b_sz, C_KV_HEADS, t // TM, LANES, TM), BF16),
                   jax.ShapeDtypeStruct((b_sz, t, D_XBC), F32),
                   jax.ShapeDtypeStruct((b_sz, t, D_INNER), F32),
                   jax.ShapeDtypeStruct((b_sz, t, LANES), F32)],
        compiler_params=_cparams(("parallel", "parallel"), 48),
        name="odd_in_proj",
    )(xs, mod, norm_g, w, qg, kg, cos, sin, bd, dtb)


def _od_attn_kernel(sink_ref, q_ref, k_ref, vt_ref, o_ref, s_ref, *, n_ctx, t, nctx_tiles):
    g = pl.program_id(1)
    i = pl.program_id(2)
    tm = q_ref.shape[0]
    w = C_WINDOW
    n_tiles = vt_ref.shape[0]
    grp = C_HEADS // C_KV_HEADS
    span = tm + 2 * w
    prev = jnp.maximum(i - 1, 0)
    nxt = jnp.minimum(i + 1, n_tiles - 1)
    k_all = jnp.concatenate([k_ref[0:n_ctx, :],
                             k_ref[pl.ds(pl.multiple_of(prev * tm + w, w), w), :],
                             k_ref[pl.ds(pl.multiple_of(i * tm, tm), tm), :],
                             k_ref[pl.ds(pl.multiple_of(nxt * tm, tm), w), :]], axis=0)
    vt_all = jnp.concatenate([vt_ref[j] for j in range(n_ctx // tm)]
                             + [vt_ref[prev][:, w:], vt_ref[i], vt_ref[nxt][:, :w]], axis=1)
    nkeys = n_ctx + span
    zero = jnp.zeros((), BF16)
    low_lane = lax.broadcasted_iota(jnp.int32, (1, LANES), 1) < C_HEAD_DIM
    low_row = lax.broadcasted_iota(jnp.int32, (LANES, 1), 0) < C_HEAD_DIM
    k2 = jnp.concatenate([jnp.where(low_lane, k_all, zero), jnp.where(low_lane, zero, k_all)], axis=0)
    vt2 = jnp.concatenate([jnp.where(low_row, vt_all, zero), jnp.where(low_row, zero, vt_all)], axis=1)
    rr = lax.broadcasted_iota(jnp.int32, (span, tm), 0)
    cc = lax.broadcasted_iota(jnp.int32, (span, tm), 1)
    kpos = i * tm - w + rr
    ok = jnp.logical_and(jnp.abs(rr - w - cc) <= w, jnp.logical_and(kpos >= n_ctx, kpos < t))
    bias = jnp.where(jnp.logical_and(ok, i >= nctx_tiles), 0.0, NEG)
    def scores(j):
        s_ref[j % 2] = _dot_nt(k2, q_ref[:, j * LANES:(j + 1) * LANES])

    scores(0)
    for j in range(grp // 2):
        if j + 1 < grp // 2:
            scores(j + 1)
        ps, invs = [], []
        for par in range(2):
            sk = sink_ref[g * grp + 2 * j + par] * LOG2E
            s_c = s_ref[j % 2, par * nkeys:par * nkeys + n_ctx, :]
            s_l = s_ref[j % 2, par * nkeys + n_ctx:(par + 1) * nkeys, :] + bias
            m = jnp.maximum(jnp.maximum(jnp.max(s_c, axis=0, keepdims=True), jnp.max(s_l, axis=0, keepdims=True)), sk)
            p_c = jnp.exp2(s_c - m)
            p_l = jnp.exp2(s_l - m)
            den = jnp.sum(p_c, axis=0, keepdims=True) + jnp.sum(p_l, axis=0, keepdims=True) + jnp.exp2(sk - m)
            ps += [p_c.astype(BF16), p_l.astype(BF16)]
            invs.append(1.0 / den)
        o_t = _dot(vt2, jnp.concatenate(ps, axis=0)) * jnp.where(low_row, invs[0], invs[1])
        o_ref[:, j * LANES:(j + 1) * LANES] = o_t.T.astype(BF16)


def _od_attn(sink, q, k, vt, n_ctx):
    b_sz, t, _ = q.shape
    assert TM == 2 * C_WINDOW
    grp_w = (C_HEADS // C_KV_HEADS) * C_HEAD_DIM
    kern = functools.partial(_od_attn_kernel, n_ctx=n_ctx, t=t, nctx_tiles=n_ctx // TM)
    return pl.pallas_call(
        kern,
        grid=(b_sz, C_KV_HEADS, t // TM),
        in_specs=[pl.BlockSpec(memory_space=pltpu.SMEM),
                  pl.BlockSpec((None, TM, grp_w), lambda b, g, i: (b, i, g)),
                  pl.BlockSpec((None, t, LANES), lambda b, g, i: (b, 0, g)),
                  pl.BlockSpec((None, None, t // TM, LANES, TM), lambda b, g, i: (b, g, 0, 0, 0))],
        out_specs=pl.BlockSpec((None, TM, grp_w), lambda b, g, i: (b, i, g)),
        out_shape=jax.ShapeDtypeStruct((b_sz, t, C_Q), BF16),
        scratch_shapes=[pltpu.VMEM((2, 2 * (n_ctx + TM + 2 * C_WINDOW), TM), F32)],
        compiler_params=_cparams(("parallel", "parallel", "arbitrary")),
        name="odd_attention",
    )(sink, q, k, vt)


def _od_conv_kernel(u_ref, up_ref, un_ref, w_ref, b_ref, xs_ref, bc_ref, buf, *, nctx_tiles, n_tiles, rb):
    i = pl.program_id(1)
    has_prev, has_next = _halo_flags(i, nctx_tiles, n_tiles)
    tm = u_ref.shape[0]
    buf[0:OD_HALO, :] = jnp.where(has_prev, up_ref[...], 0.0)
    buf[OD_HALO:OD_HALO + tm, :] = u_ref[...]
    buf[OD_HALO + tm:, :] = jnp.where(has_next, un_ref[...], 0.0)
    base = OD_HALO - D_CONV // 2
    cw = bc_ref.shape[1]
    for r in range(tm // rb):
        rows = slice(r * rb, (r + 1) * rb)
        for cb in range(D_XBC // cw):
            cs = slice(cb * cw, (cb + 1) * cw)
            acc = jnp.zeros((rb, cw), F32) + b_ref[:, cs]
            for k in range(D_CONV):
                acc = acc + buf[pl.ds(r * rb + base + k, rb), cs] * w_ref[k:k + 1, cs]
            y = _silu(acc)
            if cb * cw < D_INNER:
                xs_ref[rows, cs] = y
            else:
                bc_ref[rows, :] = y.astype(BF16)


def _od_conv(xbc, w, bias, n_ctx):
    b_sz, t, c = xbc.shape
    n_tiles = t // TM
    prev, nxt = _halo_specs(t, c, OD_HALO)
    kern = functools.partial(_od_conv_kernel, nctx_tiles=n_ctx // TM, n_tiles=n_tiles, rb=64)
    out = lambda n: pl.BlockSpec((None, TM, n), lambda b, i: (b, i, 0))
    return pl.pallas_call(
        kern,
        grid=(b_sz, n_tiles),
        in_specs=[pl.BlockSpec((None, TM, c), lambda b, i: (b, i, 0)), prev, nxt, _full(w.shape), _full(bias.shape)],
        out_specs=[out(D_INNER), out(c - D_INNER)],
        out_shape=[jax.ShapeDtypeStruct((b_sz, t, D_INNER), F32),
                   jax.ShapeDtypeStruct((b_sz, t, c - D_INNER), BF16)],
        scratch_shapes=[pltpu.VMEM((TM + 2 * OD_HALO, c), F32)],
        compiler_params=_cparams(("parallel", "parallel")),
        name="odd_ssd_conv",
    )(xbc, xbc, xbc, w, bias)


def _ssd_kernel(*refs, reverse, gated):
    if gated:
        xs_ref, bc_ref, dt_ref, av_ref, ex_ref, dsk_ref, y1_ref, z_ref, gn_ref, o_ref, h_ref = refs
    else:
        xs_ref, bc_ref, dt_ref, av_ref, ex_ref, dsk_ref, o_ref, h_ref = refs
    q = D_CHUNK
    gw = D_INNER // D_GROUPS
    hpg = D_HEADS // D_GROUPS
    col0 = D_HEADS if reverse else 0

    @pl.when(pl.program_id(1) == 0)
    def _():
        h_ref[...] = jnp.zeros_like(h_ref)

    ii = lax.broadcasted_iota(jnp.int32, (q, q), 0)
    jj = lax.broadcasted_iota(jnp.int32, (q, q), 1)
    tri = (jj >= ii) if reverse else (jj <= ii)
    tri_b = jnp.where(tri, 1.0, 0.0).astype(BF16)
    low = lax.broadcasted_iota(jnp.int32, (1, LANES), 1) < D_HEAD_DIM
    edge = 0 if reverse else q - 1

    def expand(w):
        hi, lo = _split2(w)
        return _dot(jnp.concatenate([hi, lo], axis=1), ex_ref[...])

    n_chunks = xs_ref.shape[0] // q
    for ch in (range(n_chunks - 1, -1, -1) if reverse else range(n_chunks)):
        rows = slice(ch * q, (ch + 1) * q)
        dt = dt_ref[rows, :]
        a = dt * av_ref[...]
        a1 = a.astype(BF16)
        r1 = a - a1.astype(F32)
        a2 = r1.astype(BF16)
        a3 = (r1 - a2.astype(F32)).astype(BF16)
        cs = _dot(tri_b, jnp.concatenate([a1, a2, a3], axis=1))
        acs = cs[:, 0:LANES] + cs[:, LANES:2 * LANES] + cs[:, 2 * LANES:3 * LANES]
        tot = acs[edge:edge + 1, :]
        xs = xs_ref[rows, :]
        w_y = expand(jnp.exp(acs))
        xdt = (xs * expand(dt)).astype(BF16)
        xw = (xs * expand(dt * jnp.exp(tot - acs))).astype(BF16)
        acs_t = acs.T
        bc = bc_ref[rows, :]
        h_in = h_ref[...]
        hb = h_in.astype(BF16)
        ys = []
        for g in range(D_GROUPS):
            gs = slice(g * gw, (g + 1) * gw)
            b_g = bc[:, g * D_STATE:(g + 1) * D_STATE]
            c_g = bc[:, (D_GROUPS + g) * D_STATE:(D_GROUPS + g + 1) * D_STATE]
            cb = _dot_nt(c_g, b_g)
            inter = _dot(c_g, hb[:, gs]) * w_y[:, gs]
            for pr in range(hpg // 2):
                x2 = xdt[:, g * gw + pr * LANES:g * gw + (pr + 1) * LANES]
                outs = []
                for par in range(2):
                    c = col0 + g * hpg + 2 * pr + par
                    seg = acs[:, c:c + 1] - acs_t[c:c + 1, :]
                    lmat = (jnp.exp(jnp.where(tri, seg, NEG)) * cb).astype(BF16)
                    outs.append(_dot(lmat, x2))
                ys.append(jnp.where(low, outs[0], outs[1]) + inter[:, pr * LANES:(pr + 1) * LANES])
            h_ref[:, gs] = h_in[:, gs] * w_y[edge:edge + 1, gs] + _dot_tn(b_g, xw[:, gs])
        y = jnp.concatenate(ys, axis=1)
        if gated:
            gz = (y1_ref[rows, :] + y) * _silu(z_ref[rows, :])
            ms = jnp.mean(gz * gz, axis=-1, keepdims=True)
            o_ref[rows, :] = (gz * lax.rsqrt(ms + EPS) * gn_ref[...]).astype(BF16)
        else:
            o_ref[rows, :] = dsk_ref[...] * xs + y


def _ssd(xs, bc, dt, avec, ex, dsk, n_ctx, reverse, y1=None, z=None, gn=None):
    b_sz, t, _ = xs.shape
    q = TM
    nc = t // q
    ncc = n_ctx // q
    if reverse:
        cmap = lambda b, s: (b, jnp.where(s < ncc, ncc - 1 - s, nc - 1 - (s - ncc)), 0)
    else:
        cmap = lambda b, s: (b, s, 0)
    row = lambda n: pl.BlockSpec((None, q, n), cmap)
    gated = y1 is not None
    args = [xs, bc, dt, avec, ex, dsk]
    specs = [row(D_INNER), row(bc.shape[-1]), row(LANES), _full(avec.shape), _full(ex.shape), _full(dsk.shape)]
    if gated:
        args += [y1, z, gn]
        specs += [row(D_INNER), row(D_INNER), _full(gn.shape)]
    return pl.pallas_call(
        functools.partial(_ssd_kernel, reverse=reverse, gated=gated),
        grid=(b_sz, nc),
        in_specs=specs,
        out_specs=row(D_INNER),
        out_shape=jax.ShapeDtypeStruct((b_sz, t, D_INNER), BF16 if gated else F32),
        scratch_shapes=[pltpu.VMEM((D_STATE, D_INNER), F32)],
        compiler_params=_cparams(("parallel", "arbitrary")),
        name="ssd_scan_bwd_gate" if gated else "ssd_scan_fwd",
    )(*args)


def _rope_tables(seq, n_ctx, head_dim):
    pos = jnp.arange(seq)
    row = (pos // GRID_W).astype(F32)
    col = (pos % GRID_W).astype(F32)
    quarter = head_dim // 4
    inv = ROPE_THETA ** (-jnp.arange(quarter, dtype=F32) / quarter)
    ar = row[:, None] * inv
    ac = col[:, None] * inv
    ang = jnp.concatenate([ar, ar, ac, ac], axis=-1)
    sign = jnp.where((jnp.arange(head_dim) % (head_dim // 2)) < quarter, -1.0, 1.0).astype(F32)
    cos = jnp.concatenate([jnp.ones((n_ctx, head_dim), F32), jnp.cos(ang)], axis=0)
    sin = jnp.concatenate([jnp.zeros((n_ctx, head_dim), F32), jnp.sin(ang) * sign], axis=0)
    rep = LANES // head_dim
    return jnp.tile(cos, (1, rep)), jnp.tile(sin, (1, rep))


def _dup_heads(w, head_dim):
    d, n = w.shape
    w = w.reshape(d, n // head_dim, 1, head_dim)
    return jnp.broadcast_to(w, (d, n // head_dim, 2, head_dim)).reshape(d, 2 * n)


def kernel(x, c, ctx, c_ctx, mod_w, mod_b, norm1_g, norm2_g, ffn_w_in, ffn_w_out, ev_w_in, ev_w_out, ev_q_g, ev_k_g, ev_dw_w, ev_dw_b, ev_cn_g, ev_cn_b, od_w_in, od_w_out, od_q_g, od_k_g, od_sink, od_conv_w, od_conv_b, od_dt_bias, od_A_log, od_D, od_gnorm_g):
    b_sz, seq, d = x.shape
    n_ctx = ctx.shape[1]
    depth = mod_w.shape[0]
    assert seq % TM == 0 and n_ctx % TM == 0 and seq >= TM + 2 * C_WINDOW and b_sz + 1 <= MOD_ROWS
    nctx_tiles = n_ctx // TM

    xs = (ctx, x)
    cc = jnp.zeros((MOD_ROWS, d), F32).at[:b_sz].set(c).at[b_sz].set(c_ctx)
    mod = _modulation(cc, mod_w, mod_b).reshape(depth, MOD_ROWS, 6, d)
    g1 = norm1_g.reshape(depth, 1, d)
    g2 = norm2_g.reshape(depth, 1, d)

    cos_a, sin_a = _rope_tables(seq, n_ctx, A_HEAD_DIM)
    cos_c, sin_c = _rope_tables(seq, n_ctx, C_HEAD_DIM)
    lane = jnp.arange(LANES)
    bd = (lane[:, None] // C_HEAD_DIM == lane[None, :] // C_HEAD_DIM).astype(BF16)
    bd = jnp.concatenate([bd, bd], axis=0)
    head_of_lane = jnp.arange(D_INNER) // D_HEAD_DIM
    ex = []
    for dirn in range(2):
        e = (lane[:, None] == head_of_lane[None, :] + dirn * D_HEADS).astype(BF16)
        ex.append(jnp.concatenate([e, e], axis=0))

    for l in range(depth):
        i = l // 2
        wi = ffn_w_in[l].astype(BF16)
        wo = ffn_w_out[l].astype(BF16)
        if l % 2 == 0:
            w_in = ev_w_in[i].astype(BF16)
            w_out = ev_w_out[i].astype(BF16)
            q, k, v, u = _ev_in(xs, mod, g1, w_in, ev_q_g[i].reshape(1, -1), ev_k_g[i].reshape(1, -1),
                                cos_a, sin_a, l, nctx_tiles)
            a1 = _ev_attn(q, k, v, n_ctx)
            a2 = _ev_conv(u, ev_dw_w[i], ev_dw_b[i].reshape(1, -1), ev_cn_g[i].reshape(1, -1),
                          ev_cn_b[i].reshape(1, -1), n_ctx)
            wo1, wo2 = w_out[:A_Q], w_out[A_Q:]
        else:
            w = od_w_in[i]
            o_k, o_v, o_x, o_dt, o_z = C_Q, C_Q + C_KV, C_Q + 2 * C_KV, C_Q + 2 * C_KV + D_XBC, C_Q + 2 * C_KV + D_XBC + 2 * D_HEADS
            w_in = jnp.concatenate([
                w[:, :o_k], _dup_heads(w[:, o_k:o_v], C_HEAD_DIM), _dup_heads(w[:, o_v:o_x], C_HEAD_DIM),
                w[:, o_x:o_dt], w[:, o_z:], w[:, o_dt:o_z], jnp.zeros((d, LANES - 2 * D_HEADS), F32)], axis=1).astype(BF16)
            w_out = od_w_out[i].astype(BF16)
            pad = jnp.zeros((LANES - 2 * D_HEADS,), F32)
            dtb = jnp.concatenate([od_dt_bias[i].reshape(-1), pad]).reshape(1, LANES)
            avec = jnp.concatenate([-jnp.exp(od_A_log[i].astype(F32)).reshape(-1), pad]).reshape(1, LANES)
            dsk = jnp.repeat(od_D[i].astype(F32), D_HEAD_DIM).reshape(1, D_INNER)
            qg = jnp.tile(od_q_g[i], LANES // C_HEAD_DIM).reshape(1, LANES)
            kg = jnp.tile(od_k_g[i], LANES // C_HEAD_DIM).reshape(1, LANES)
            q, k, v, xbc, z, dt = _od_in(xs, mod, g1, w_in, qg, kg, cos_c, sin_c, bd, dtb, l, nctx_tiles)
            a1 = _od_attn(od_sink[i], q, k, v, n_ctx)
            xc, bc = _od_conv(xbc, od_conv_w[i], od_conv_b[i].reshape(1, -1), n_ctx)
            y1 = _ssd(xc, bc, dt, avec, ex[0], dsk, n_ctx, reverse=False)
            a2 = _ssd(xc, bc, dt, avec, ex[1], dsk, n_ctx, reverse=True, y1=y1, z=z,
                      gn=od_gnorm_g[i].reshape(1, -1))
            wo1, wo2 = w_out[:C_Q], w_out[C_Q:]
        xs = _out_ffn(xs, a1, a2, mod, g2, wo1, wo2, wi, wo, l, nctx_tiles, latent_only=(l == depth - 1))
    return xs
```

```python
import functools
import math

import jax
import jax.numpy as jnp
from jax import lax
from jax.experimental import pallas as pl
from jax.experimental.pallas import tpu as pltpu

F32 = jnp.float32
BF16 = jnp.bfloat16

EPS = 1e-6
ROPE_THETA = 10000.0
GRID_W = 64
LOG2E = math.log2(math.e)
NEG = -1e30

A_HEADS, A_KV_HEADS, A_HEAD_DIM = 8, 2, 128
B_WIDTH, B_CONV = 512, 31
C_HEADS, C_KV_HEADS, C_HEAD_DIM, C_WINDOW = 16, 2, 64, 128
D_INNER, D_HEAD_DIM, D_HEADS, D_STATE, D_GROUPS, D_CONV, D_CHUNK = 1024, 64, 16, 128, 2, 5, 128
A_Q, A_KV = A_HEADS * A_HEAD_DIM, A_KV_HEADS * A_HEAD_DIM
C_Q, C_KV = C_HEADS * C_HEAD_DIM, C_KV_HEADS * C_HEAD_DIM
D_XBC = D_INNER + 2 * D_GROUPS * D_STATE

LANES = 128
SUBLANES = 8
VMEM_BYTES = 64 * 1024 * 1024

TM = 256
MOD_ROWS = 16
EV_HALO = 16
OD_HALO = 8


def _cparams(sem, vmem_mb=None):
    kw = dict(dimension_semantics=sem)
    if vmem_mb is not None:
        kw["vmem_limit_bytes"] = vmem_mb * 1024 * 1024
    return pltpu.CompilerParams(**kw)


def _dot(a, b):
    return jnp.dot(a, b, preferred_element_type=F32)


def _dot_nt(a, b):
    return lax.dot_general(a, b, (((1,), (1,)), ((), ())), preferred_element_type=F32)


def _dot_tn(a, b):
    return lax.dot_general(a, b, (((0,), (0,)), ((), ())), preferred_element_type=F32)


def _silu(x):
    return x * jax.nn.sigmoid(x)


def _split2(x):
    hi = x.astype(BF16)
    lo = (x - hi.astype(F32)).astype(BF16)
    return hi, lo


def _prenorm(x, g, shift, scale):
    ms = jnp.mean(x * x, axis=-1, keepdims=True)
    return (x * lax.rsqrt(ms + EPS) * g) * (1.0 + scale) + shift


def _mod_kernel(c_ref, w_ref, b_ref, o_ref):
    a = _silu(c_ref[...]).astype(BF16)
    o_ref[...] = _dot(a, w_ref[...].astype(BF16)) + b_ref[...]


def _modulation(cc, mod_w, mod_b):
    depth, d, n = mod_w.shape
    tn = 6 * LANES * 2
    return pl.pallas_call(
        _mod_kernel,
        grid=(depth, n // tn),
        in_specs=[pl.BlockSpec((MOD_ROWS, d), lambda l, j: (0, 0)),
                  pl.BlockSpec((None, d, tn), lambda l, j: (l, 0, j)),
                  pl.BlockSpec((None, 1, tn), lambda l, j: (l, 0, j))],
        out_specs=pl.BlockSpec((None, MOD_ROWS, tn), lambda l, j: (l, 0, j)),
        out_shape=jax.ShapeDtypeStruct((depth, MOD_ROWS, n), F32),
        compiler_params=_cparams(("parallel", "parallel")),
        name="modulation",
    )(cc, mod_w, mod_b.reshape(depth, 1, n))


def _ev_in_kernel(*refs, n_src, nctx_tiles):
    mod_ref, g_ref, w_ref, qg_ref, kg_ref, cos_ref, sin_ref, q_ref, k_ref, vt_ref, u_ref = refs[n_src:]
    x = _load_stream(refs[:n_src], nctx_tiles)
    hb = _prenorm(x, g_ref[...], mod_ref[0:1, :], mod_ref[1:2, :]).astype(BF16)
    cos = cos_ref[...]
    sin = sin_ref[...]
    lane = lax.broadcasted_iota(jnp.int32, (1, LANES), 1)
    first = (lane % (A_HEAD_DIM // 2)) < (A_HEAD_DIM // 4)

    def norm_rope(t, g):
        y = t * lax.rsqrt(jnp.mean(t * t, axis=-1, keepdims=True) + EPS) * g
        rot = jnp.where(first, pltpu.roll(y, LANES - A_HEAD_DIM // 4, 1), pltpu.roll(y, A_HEAD_DIM // 4, 1))
        return y * cos + rot * sin

    qkv = _dot(hb, w_ref[:, 0:A_Q + 2 * A_KV])
    qg = qg_ref[...] * (A_HEAD_DIM ** -0.5 * LOG2E)
    for h in range(A_HEADS):
        sl = slice(h * LANES, (h + 1) * LANES)
        q_ref[:, sl] = norm_rope(qkv[:, sl], qg).astype(BF16)
    for h in range(A_KV_HEADS):
        sl = slice(h * LANES, (h + 1) * LANES)
        k_ref[:, sl] = norm_rope(qkv[:, A_Q + h * LANES:A_Q + (h + 1) * LANES], kg_ref[...]).astype(BF16)
    for h in range(A_KV_HEADS):
        vt_ref[h] = qkv[:, A_Q + A_KV + h * LANES:A_Q + A_KV + (h + 1) * LANES].T.astype(BF16)
    glu = _dot(hb, w_ref[:, A_Q + 2 * A_KV:])
    u_ref[...] = glu[:, :B_WIDTH] * jax.nn.sigmoid(glu[:, B_WIDTH:])


def _row_specs(b_sz, nctx_tiles, d, layer, first_tile=0):
    mod_spec = pl.BlockSpec((None, None, 6, d),
                            lambda b, i: (layer, jnp.where(i + first_tile < nctx_tiles, b_sz, b), 0, 0))
    g_spec = pl.BlockSpec((None, 1, d), lambda b, i: (layer, 0, 0))
    return mod_spec, g_spec


def _stream_specs(src, nctx_tiles, first_tile=0):
    if not isinstance(src, tuple):
        d = src.shape[-1]
        return [pl.BlockSpec((None, TM, d), lambda b, i: (b, i + first_tile, 0))], [src], src.shape
    ctx, x = src
    d = x.shape[-1]
    c_spec = pl.BlockSpec((None, TM, d), lambda b, i: (b, jnp.minimum(i + first_tile, nctx_tiles - 1), 0))
    x_spec = pl.BlockSpec((None, TM, d), lambda b, i: (b, jnp.maximum(i + first_tile - nctx_tiles, 0), 0))
    return [c_spec, x_spec], [ctx, x], (x.shape[0], ctx.shape[1] + x.shape[1], d)


def _load_stream(src_refs, nctx_tiles, first_tile=0):
    if len(src_refs) == 1:
        return src_refs[0][...]
    return jnp.where(pl.program_id(1) + first_tile < nctx_tiles, src_refs[0][...], src_refs[1][...])


def _full(shape):
    nd = len(shape)
    return pl.BlockSpec(shape, lambda *_: (0,) * nd, pipeline_mode=pl.Buffered(1))


def _ev_in(src, mod, norm_g, w, qg, kg, cos, sin, layer, nctx_tiles):
    src_specs, src_arrays, (b_sz, t, d) = _stream_specs(src, nctx_tiles)
    mod_spec, g_spec = _row_specs(b_sz, nctx_tiles, d, layer)
    tab = pl.BlockSpec((TM, LANES), lambda b, i: (i, 0))
    out = lambda n: pl.BlockSpec((None, TM, n), lambda b, i: (b, i, 0))
    return pl.pallas_call(
        functools.partial(_ev_in_kernel, n_src=len(src_arrays), nctx_tiles=nctx_tiles),
        grid=(b_sz, t // TM),
        in_specs=src_specs + [mod_spec, g_spec, _full(w.shape), _full(qg.shape), _full(kg.shape), tab, tab],
        out_specs=[out(A_Q), out(A_KV),
                   pl.BlockSpec((None, A_KV_HEADS, None, A_HEAD_DIM, TM), lambda b, i: (b, 0, i, 0, 0)),
                   out(B_WIDTH)],
        out_shape=[jax.ShapeDtypeStruct((b_sz, t, A_Q), BF16),
                   jax.ShapeDtypeStruct((b_sz, t, A_KV), BF16),
                   jax.ShapeDtypeStruct((b_sz, A_KV_HEADS, t // TM, A_HEAD_DIM, TM), BF16),
                   jax.ShapeDtypeStruct((b_sz, t, B_WIDTH), F32)],
        compiler_params=_cparams(("parallel", "parallel"), 48),
        name="even_in_proj",
    )(*src_arrays, mod, norm_g, w, qg, kg, cos, sin)


def _ev_attn_kernel(q_ref, k_ref, vt_ref, o_ref, acc_ref, s_ref, *, kb_per_step, n_ctx, nctx_tiles):
    i = pl.program_id(2)
    tm = q_ref.shape[0]
    kb = vt_ref.shape[-1]
    grp = A_HEADS // A_KV_HEADS
    m_rows = grp * tm
    q = jnp.concatenate([q_ref[:, j * LANES:(j + 1) * LANES] for j in range(grp)], axis=0)

    def softmax_pv(s, vt, m, l, first):
        m_new = jnp.maximum(m, jnp.max(s, axis=0, keepdims=True))
        p = jnp.exp2(s - m_new)
        alpha = jnp.exp2(m - m_new)
        l_new = alpha * l + jnp.sum(p, axis=0, keepdims=True)
        pv = _dot(vt, p.astype(BF16))
        acc_ref[...] = pv if first else acc_ref[...] * alpha + pv
        return m_new, l_new

    def finish(l):
        o = (acc_ref[...] / l).T
        for j in range(grp):
            o_ref[:, j * LANES:(j + 1) * LANES] = o[j * tm:(j + 1) * tm].astype(BF16)

    nb_ctx = n_ctx // kb

    def ctx_block(s_ctx):
        vt = jnp.concatenate([vt_ref[j] for j in range(nb_ctx)], axis=1)
        return softmax_pv(s_ctx, vt, jnp.full((1, m_rows), NEG, F32), jnp.zeros((1, m_rows), F32), True)

    @pl.when(i < nctx_tiles)
    def _():
        _, l_ctx = ctx_block(_dot_nt(k_ref[0:n_ctx, :], q))
        finish(l_ctx)

    kps = kb_per_step

    def scores(blk, slot):
        off = pl.multiple_of(blk * kb, kb)
        s_ref[slot, :, 0:m_rows] = _dot_nt(k_ref[pl.ds(off, kps * kb), :], q)

    def consume(blk, slot, carry):
        vt = jnp.concatenate([vt_ref[blk + r] for r in range(kps)], axis=1)
        return softmax_pv(s_ref[slot, :, 0:m_rows], vt, *carry, False)

    n_pairs = (vt_ref.shape[0] - nb_ctx) // (2 * kps)

    @pl.when(i >= nctx_tiles)
    def _():
        s_ctx = _dot_nt(k_ref[0:n_ctx, :], q)
        scores(nb_ctx, 0)
        m, l = ctx_block(s_ctx)

        def body(jj, carry):
            blk = nb_ctx + 2 * kps * jj
            scores(blk + kps, 1)
            carry = consume(blk, 0, carry)
            scores(blk + 2 * kps, 0)
            return consume(blk + kps, 1, carry)

        carry = lax.fori_loop(0, n_pairs - 1, body, (m, l))
        blk = nb_ctx + 2 * kps * (n_pairs - 1)
        scores(blk + kps, 1)
        carry = consume(blk, 0, carry)
        _, l_fin = consume(blk + kps, 1, carry)
        finish(l_fin)


def _ev_attn(q, k, vt, n_ctx):
    b_sz, t, _ = q.shape
    nblk, kb = vt.shape[2], vt.shape[4]
    kb_per_step = 4
    assert (nblk - n_ctx // kb) % (2 * kb_per_step) == 0
    grp = A_HEADS // A_KV_HEADS
    grp_w = grp * A_HEAD_DIM
    kern = functools.partial(_ev_attn_kernel, kb_per_step=kb_per_step, n_ctx=n_ctx, nctx_tiles=n_ctx // TM)
    return pl.pallas_call(
        kern,
        grid=(b_sz, A_KV_HEADS, t // TM),
        in_specs=[pl.BlockSpec((None, TM, grp_w), lambda b, g, i: (b, i, g)),
                  pl.BlockSpec((None, t, A_HEAD_DIM), lambda b, g, i: (b, 0, g)),
                  pl.BlockSpec((None, None, nblk, A_HEAD_DIM, kb), lambda b, g, i: (b, g, 0, 0, 0))],
        out_specs=pl.BlockSpec((None, TM, grp_w), lambda b, g, i: (b, i, g)),
        out_shape=jax.ShapeDtypeStruct((b_sz, t, A_Q), BF16),
        scratch_shapes=[pltpu.VMEM((A_HEAD_DIM, grp * TM), F32),
                        pltpu.VMEM((2, kb_per_step * kb, grp * TM + LANES), F32)],
        compiler_params=_cparams(("parallel", "parallel", "arbitrary"), 48),
        name="even_attention",
    )(q, k, vt)


def _halo_flags(i, nctx_tiles, n_tiles):
    has_prev = jnp.logical_and(i != 0, i != nctx_tiles)
    has_next = jnp.logical_and(i != nctx_tiles - 1, i != n_tiles - 1)
    return has_prev, has_next


def _ev_conv_kernel(u_ref, up_ref, un_ref, w_ref, b_ref, g_ref, bb_ref, o_ref, buf, sh, *, nctx_tiles, n_tiles, rb):
    i = pl.program_id(1)
    has_prev, has_next = _halo_flags(i, nctx_tiles, n_tiles)
    tm = u_ref.shape[0]
    buf[0:EV_HALO, :] = jnp.where(has_prev, up_ref[...], 0.0)
    buf[EV_HALO:EV_HALO + tm, :] = u_ref[...]
    buf[EV_HALO + tm:, :] = jnp.where(has_next, un_ref[...], 0.0)
    for s in range(1, SUBLANES):
        sh[s - 1] = buf[pl.ds(s, sh.shape[1]), :]
    base = EV_HALO - B_CONV // 2
    for r in range(tm // rb):
        acc = jnp.zeros((rb, B_WIDTH), F32) + b_ref[...]
        for k in range(B_CONV):
            s = (base + k) % SUBLANES
            al = r * rb + base + k - s
            src = buf[al:al + rb, :] if s == 0 else sh[s - 1, al:al + rb, :]
            acc = acc + src * w_ref[k:k + 1, :]
        mu = jnp.mean(acc, axis=-1, keepdims=True)
        cen = acc - mu
        var = jnp.mean(cen * cen, axis=-1, keepdims=True)
        y = cen * lax.rsqrt(var + EPS) * g_ref[...] + bb_ref[...]
        o_ref[r * rb:(r + 1) * rb, :] = _silu(y).astype(BF16)


def _halo_specs(t, width, halo):
    per = TM // halo
    last = t // halo - 1
    prev = pl.BlockSpec((None, halo, width), lambda b, i: (b, jnp.maximum(i * per - 1, 0), 0))
    nxt = pl.BlockSpec((None, halo, width), lambda b, i: (b, jnp.minimum((i + 1) * per, last), 0))
    return prev, nxt


def _ev_conv(u, w, bias, g, bb, n_ctx):
    b_sz, t, c = u.shape
    n_tiles = t // TM
    prev, nxt = _halo_specs(t, c, EV_HALO)
    kern = functools.partial(_ev_conv_kernel, nctx_tiles=n_ctx // TM, n_tiles=n_tiles, rb=32)
    return pl.pallas_call(
        kern,
        grid=(b_sz, n_tiles),
        in_specs=[pl.BlockSpec((None, TM, c), lambda b, i: (b, i, 0)), prev, nxt,
                  _full(w.shape), _full(bias.shape), _full(g.shape), _full(bb.shape)],
        out_specs=pl.BlockSpec((None, TM, c), lambda b, i: (b, i, 0)),
        out_shape=jax.ShapeDtypeStruct((b_sz, t, c), BF16),
        scratch_shapes=[pltpu.VMEM((TM + 2 * EV_HALO, c), F32),
                        pltpu.VMEM((SUBLANES - 1, TM + 2 * EV_HALO - SUBLANES, c), F32)],
        compiler_params=_cparams(("parallel", "parallel")),
        name="even_conv_module",
    )(u, u, u, w, bias, g, bb)


def _out_ffn_kernel(*refs, n_src, nctx_tiles, first_tile, hid, chunk):
    a1_ref, a2_ref, mod_ref, g_ref, wo1_ref, wo2_ref, wi_ref, wo_ref, o_ref, act_ref = refs[n_src:]
    o = _dot(a1_ref[...], wo1_ref[...]) + _dot(a2_ref[...], wo2_ref[...])
    x1 = _load_stream(refs[:n_src], nctx_tiles, first_tile) + mod_ref[2:3, :] * o
    h2 = _prenorm(x1, g_ref[...], mod_ref[3:4, :], mod_ref[4:5, :]).astype(BF16)
    for c in range(hid // chunk):
        gte = _dot(h2, wi_ref[:, c * chunk:(c + 1) * chunk])
        up = _dot(h2, wi_ref[:, hid + c * chunk:hid + (c + 1) * chunk])
        act_ref[:, c * chunk:(c + 1) * chunk] = (_silu(gte) * up).astype(BF16)
    o_ref[...] = x1 + mod_ref[5:6, :] * _dot(act_ref[...], wo_ref[...])


def _out_ffn(src, a1, a2, mod, norm_g, wo1, wo2, wi, wo, layer, nctx_tiles, latent_only):
    first = nctx_tiles if latent_only else 0
    src_specs, src_arrays, (b_sz, t, d) = _stream_specs(src, nctx_tiles, first)
    hid = wo.shape[0]
    n_tiles = t // TM - first
    mod_spec, g_spec = _row_specs(b_sz, nctx_tiles, d, layer, first)
    row = lambda n: pl.BlockSpec((None, TM, n), lambda b, i: (b, i + first, 0))
    kern = functools.partial(_out_ffn_kernel, n_src=len(src_arrays), nctx_tiles=nctx_tiles, first_tile=first,
                             hid=hid, chunk=2 * LANES)
    in_place = len(src_arrays) == 1 and not latent_only
    return pl.pallas_call(
        kern,
        grid=(b_sz, n_tiles),
        in_specs=src_specs + [row(a1.shape[-1]), row(a2.shape[-1]), mod_spec, g_spec,
                              _full(wo1.shape), _full(wo2.shape), _full(wi.shape), _full(wo.shape)],
        out_specs=pl.BlockSpec((None, TM, d), lambda b, i: (b, i, 0)),
        out_shape=jax.ShapeDtypeStruct((b_sz, n_tiles * TM, d), F32),
        scratch_shapes=[pltpu.VMEM((TM, hid), BF16)],
        input_output_aliases={0: 0} if in_place else {},
        compiler_params=_cparams(("parallel", "parallel"), 56),
        name="out_proj_ffn",
    )(*src_arrays, a1, a2, mod, norm_g, wo1, wo2, wi, wo)


def _od_in_kernel(x_ref, mod_ref, g_ref, w_ref, qg_ref, kg_ref, cos_ref, sin_ref, bd_ref, dtb_ref,
                  q_ref, k_ref, vt_ref, xbc_ref, z_ref, dt_ref):
    hb = _prenorm(x_ref[...], g_ref[...], mod_ref[0:1, :], mod_ref[1:2, :]).astype(BF16)
    cos = cos_ref[...]
    sin = sin_ref[...]
    bd = bd_ref[...]
    lane = lax.broadcasted_iota(jnp.int32, (1, LANES), 1)
    first = (lane % (C_HEAD_DIM // 2)) < (C_HEAD_DIM // 4)

    def norm_rope(t, g):
        hi, lo = _split2(t * t)
        ss = _dot(jnp.concatenate([hi, lo], axis=1), bd)
        y = t * lax.rsqrt(ss * (1.0 / C_HEAD_DIM) + EPS) * g
        rot = jnp.where(first, pltpu.roll(y, LANES - C_HEAD_DIM // 4, 1), pltpu.roll(y, C_HEAD_DIM // 4, 1))
        return y * cos + rot * sin

    nq = C_Q // LANES
    nk = 2 * C_KV // LANES
    qkv = _dot(hb, w_ref[:, 0:C_Q + 4 * C_KV])
    qg = qg_ref[...] * (C_HEAD_DIM ** -0.5 * LOG2E)
    for j in range(nq):
        sl = slice(j * LANES, (j + 1) * LANES)
        q_ref[:, sl] = norm_rope(qkv[:, sl], qg).astype(BF16)
    for j in range(nk):
        k_ref[:, j * LANES:(j + 1) * LANES] = norm_rope(qkv[:, C_Q + j * LANES:C_Q + (j + 1) * LANES], kg_ref[...]).astype(BF16)
    for j in range(nk):
        vt_ref[j] = qkv[:, C_Q + 2 * C_KV + j * LANES:C_Q + 2 * C_KV + (j + 1) * LANES].T.astype(BF16)
    rest = _dot(hb, w_ref[:, C_Q + 4 * C_KV:])
    xbc_ref[...] = rest[:, 0:D_XBC]
    z_ref[...] = rest[:, D_XBC:D_XBC + D_INNER]
    dtr = rest[:, D_XBC + D_INNER:] + dtb_ref[...]
    dt_ref[...] = jnp.maximum(dtr, 0.0) + jnp.log1p(jnp.exp(-jnp.abs(dtr)))


def _od_in(xs, mod, norm_g, w, qg, kg, cos, sin, bd, dtb, layer, nctx_tiles):
    b_sz, t, d = xs.shape
    x_spec = pl.BlockSpec((None, TM, d), lambda b, i: (b, i, 0))
    mod_spec, g_spec = _row_specs(b_sz, nctx_tiles, d, layer)
    tab = pl.BlockSpec((TM, LANES), lambda b, i: (i, 0))
    out = lambda n: pl.BlockSpec((None, TM, n), lambda b, i: (b, i, 0))
    return pl.pallas_call(
        _od_in_kernel,
        grid=(b_sz, t // TM),
        in_specs=[x_spec, mod_spec, g_spec, _full(w.shape), _full(qg.shape), _full(kg.shape), tab, tab,
                  _full(bd.shape), _full(dtb.shape)],
        out_specs=[out(C_Q), out(2 * C_KV),
                   pl.BlockSpec((None, C_KV_HEADS, None, LANES, TM), lambda b, i: (b, 0, i, 0, 0)),
                   out(D_XBC), out(D_INNER), out(LANES)],
        out_shape=[jax.ShapeDtypeStruct((b_sz, t, C_Q), BF16),
                   jax.ShapeDtypeStruct((b_sz, t, 2 * C_KV), BF16),
                   jax.ShapeDtypeStruct((b_sz, C_KV_HEADS, t // TM, LANES, TM), BF16),
                   jax.ShapeDtypeStruct((b_sz, t, D_XBC), F32),
                   jax.ShapeDtypeStruct((b_sz, t, D_INNER), F32),
                   jax.ShapeDtypeStruct((b_sz, t, LANES), F32)],
        compiler_params=_cparams(("parallel", "parallel"), 48),
        name="odd_in_proj",
    )(xs, mod, norm_g, w, qg, kg, cos, sin, bd, dtb)


def _od_attn_kernel(sink_ref, q_ref, k_ref, vt_ref, o_ref, s_ref, *, n_ctx, t, nctx_tiles):
    g = pl.program_id(1)
    i = pl.program_id(2)
    tm = q_ref.shape[0]
    w = C_WINDOW
    n_tiles = vt_ref.shape[0]
    grp = C_HEADS // C_KV_HEADS
    span = tm + 2 * w
    prev = jnp.maximum(i - 1, 0)
    nxt = jnp.minimum(i + 1, n_tiles - 1)
    k_all = jnp.concatenate([k_ref[0:n_ctx, :],
                             k_ref[pl.ds(pl.multiple_of(prev * tm + w, w), w), :],
                             k_ref[pl.ds(pl.multiple_of(i * tm, tm), tm), :],
                             k_ref[pl.ds(pl.multiple_of(nxt * tm, tm), w), :]], axis=0)
    vt_all = jnp.concatenate([vt_ref[j] for j in range(n_ctx // tm)]
                             + [vt_ref[prev][:, w:], vt_ref[i], vt_ref[nxt][:, :w]], axis=1)
    nkeys = n_ctx + span
    zero = jnp.zeros((), BF16)
    low_lane = lax.broadcasted_iota(jnp.int32, (1, LANES), 1) < C_HEAD_DIM
    low_row = lax.broadcasted_iota(jnp.int32, (LANES, 1), 0) < C_HEAD_DIM
    k2 = jnp.concatenate([jnp.where(low_lane, k_all, zero), jnp.where(low_lane, zero, k_all)], axis=0)
    vt2 = jnp.concatenate([jnp.where(low_row, vt_all, zero), jnp.where(low_row, zero, vt_all)], axis=1)
    rr = lax.broadcasted_iota(jnp.int32, (span, tm), 0)
    cc = lax.broadcasted_iota(jnp.int32, (span, tm), 1)
    kpos = i * tm - w + rr
    ok = jnp.logical_and(jnp.abs(rr - w - cc) <= w, jnp.logical_and(kpos >= n_ctx, kpos < t))
    bias = jnp.where(jnp.logical_and(ok, i >= nctx_tiles), 0.0, NEG)
    def scores(j):
        s_ref[j % 2, :, 0:tm] = _dot_nt(k2, q_ref[:, j * LANES:(j + 1) * LANES])

    scores(0)
    for j in range(grp // 2):
        if j + 1 < grp // 2:
            scores(j + 1)
        ps, invs = [], []
        for par in range(2):
            sk = sink_ref[g * grp + 2 * j + par] * LOG2E
            s_c = s_ref[j % 2, par * nkeys:par * nkeys + n_ctx, 0:tm]
            s_l = s_ref[j % 2, par * nkeys + n_ctx:(par + 1) * nkeys, 0:tm] + bias
            m = jnp.maximum(jnp.maximum(jnp.max(s_c, axis=0, keepdims=True), jnp.max(s_l, axis=0, keepdims=True)), sk)
            p_c = jnp.exp2(s_c - m)
            p_l = jnp.exp2(s_l - m)
            den = jnp.sum(p_c, axis=0, keepdims=True) + jnp.sum(p_l, axis=0, keepdims=True) + jnp.exp2(sk - m)
            ps += [p_c.astype(BF16), p_l.astype(BF16)]
            invs.append(1.0 / den)
        o_t = _dot(vt2, jnp.concatenate(ps, axis=0)) * jnp.where(low_row, invs[0], invs[1])
        o_ref[:, j * LANES:(j + 1) * LANES] = o_t.T.astype(BF16)


def _od_attn(sink, q, k, vt, n_ctx):
    b_sz, t, _ = q.shape
    assert TM == 2 * C_WINDOW
    grp_w = (C_HEADS // C_KV_HEADS) * C_HEAD_DIM
    kern = functools.partial(_od_attn_kernel, n_ctx=n_ctx, t=t, nctx_tiles=n_ctx // TM)
    return pl.pallas_call(
        kern,
        grid=(b_sz, C_KV_HEADS, t // TM),
        in_specs=[pl.BlockSpec(memory_space=pltpu.SMEM),
                  pl.BlockSpec((None, TM, grp_w), lambda b, g, i: (b, i, g)),
                  pl.BlockSpec((None, t, LANES), lambda b, g, i: (b, 0, g)),
                  pl.BlockSpec((None, None, t // TM, LANES, TM), lambda b, g, i: (b, g, 0, 0, 0))],
        out_specs=pl.BlockSpec((None, TM, grp_w), lambda b, g, i: (b, i, g)),
        out_shape=jax.ShapeDtypeStruct((b_sz, t, C_Q), BF16),
        scratch_shapes=[pltpu.VMEM((2, 2 * (n_ctx + TM + 2 * C_WINDOW), TM + LANES), F32)],
        compiler_params=_cparams(("parallel", "parallel", "arbitrary")),
        name="odd_attention",
    )(sink, q, k, vt)


def _od_conv_kernel(u_ref, up_ref, un_ref, w_ref, b_ref, xs_ref, bc_ref, buf, *, nctx_tiles, n_tiles, rb):
    i = pl.program_id(1)
    has_prev, has_next = _halo_flags(i, nctx_tiles, n_tiles)
    tm = u_ref.shape[0]
    buf[0:OD_HALO, :] = jnp.where(has_prev, up_ref[...], 0.0)
    buf[OD_HALO:OD_HALO + tm, :] = u_ref[...]
    buf[OD_HALO + tm:, :] = jnp.where(has_next, un_ref[...], 0.0)
    base = OD_HALO - D_CONV // 2
    cw = bc_ref.shape[1]
    for r in range(tm // rb):
        rows = slice(r * rb, (r + 1) * rb)
        for cb in range(D_XBC // cw):
            cs = slice(cb * cw, (cb + 1) * cw)
            acc = jnp.zeros((rb, cw), F32) + b_ref[:, cs]
            for k in range(D_CONV):
                acc = acc + buf[pl.ds(r * rb + base + k, rb), cs] * w_ref[k:k + 1, cs]
            y = _silu(acc)
            if cb * cw < D_INNER:
                xs_ref[rows, cs] = y
            else:
                bc_ref[rows, :] = y.astype(BF16)


def _od_conv(xbc, w, bias, n_ctx):
    b_sz, t, c = xbc.shape
    n_tiles = t // TM
    prev, nxt = _halo_specs(t, c, OD_HALO)
    kern = functools.partial(_od_conv_kernel, nctx_tiles=n_ctx // TM, n_tiles=n_tiles, rb=64)
    out = lambda n: pl.BlockSpec((None, TM, n), lambda b, i: (b, i, 0))
    return pl.pallas_call(
        kern,
        grid=(b_sz, n_tiles),
        in_specs=[pl.BlockSpec((None, TM, c), lambda b, i: (b, i, 0)), prev, nxt, _full(w.shape), _full(bias.shape)],
        out_specs=[out(D_INNER), out(c - D_INNER)],
        out_shape=[jax.ShapeDtypeStruct((b_sz, t, D_INNER), F32),
                   jax.ShapeDtypeStruct((b_sz, t, c - D_INNER), BF16)],
        scratch_shapes=[pltpu.VMEM((TM + 2 * OD_HALO, c), F32)],
        compiler_params=_cparams(("parallel", "parallel")),
        name="odd_ssd_conv",
    )(xbc, xbc, xbc, w, bias)


def _ssd_kernel(*refs, reverse, gated):
    if gated:
        xs_ref, bc_ref, dt_ref, av_ref, ex_ref, dsk_ref, y1_ref, z_ref, gn_ref, o_ref, h_ref = refs
    else:
        xs_ref, bc_ref, dt_ref, av_ref, ex_ref, dsk_ref, o_ref, h_ref = refs
    q = D_CHUNK
    gw = D_INNER // D_GROUPS
    hpg = D_HEADS // D_GROUPS
    col0 = D_HEADS if reverse else 0

    @pl.when(pl.program_id(1) == 0)
    def _():
        h_ref[...] = jnp.zeros_like(h_ref)

    ii = lax.broadcasted_iota(jnp.int32, (q, q), 0)
    jj = lax.broadcasted_iota(jnp.int32, (q, q), 1)
    tri = (jj >= ii) if reverse else (jj <= ii)
    tri_b = jnp.where(tri, 1.0, 0.0).astype(BF16)
    low = lax.broadcasted_iota(jnp.int32, (1, LANES), 1) < D_HEAD_DIM
    edge = 0 if reverse else q - 1

    def expand(w):
        hi, lo = _split2(w)
        return _dot(jnp.concatenate([hi, lo], axis=1), ex_ref[...])

    n_chunks = xs_ref.shape[0] // q
    for ch in (range(n_chunks - 1, -1, -1) if reverse else range(n_chunks)):
        rows = slice(ch * q, (ch + 1) * q)
        dt = dt_ref[rows, :]
        a = dt * av_ref[...]
        a1 = a.astype(BF16)
        r1 = a - a1.astype(F32)
        a2 = r1.astype(BF16)
        a3 = (r1 - a2.astype(F32)).astype(BF16)
        cs = _dot(tri_b, jnp.concatenate([a1, a2, a3], axis=1))
        acs = cs[:, 0:LANES] + cs[:, LANES:2 * LANES] + cs[:, 2 * LANES:3 * LANES]
        tot = acs[edge:edge + 1, :]
        xs = xs_ref[rows, :]
        w_y = expand(jnp.exp(acs))
        xdt = (xs * expand(dt)).astype(BF16)
        xw = (xs * expand(dt * jnp.exp(tot - acs))).astype(BF16)
        acs_t = acs.T
        bc = bc_ref[rows, :]
        h_in = h_ref[...]
        hb = h_in.astype(BF16)
        ys = []
        for g in range(D_GROUPS):
            gs = slice(g * gw, (g + 1) * gw)
            b_g = bc[:, g * D_STATE:(g + 1) * D_STATE]
            c_g = bc[:, (D_GROUPS + g) * D_STATE:(D_GROUPS + g + 1) * D_STATE]
            cb = _dot_nt(c_g, b_g)
            inter = _dot(c_g, hb[:, gs]) * w_y[:, gs]
            for pr in range(hpg // 2):
                x2 = xdt[:, g * gw + pr * LANES:g * gw + (pr + 1) * LANES]
                outs = []
                for par in range(2):
                    c = col0 + g * hpg + 2 * pr + par
                    seg = acs[:, c:c + 1] - acs_t[c:c + 1, :]
                    lmat = (jnp.exp(jnp.where(tri, seg, NEG)) * cb).astype(BF16)
                    outs.append(_dot(lmat, x2))
                ys.append(jnp.where(low, outs[0], outs[1]) + inter[:, pr * LANES:(pr + 1) * LANES])
            h_ref[:, gs] = h_in[:, gs] * w_y[edge:edge + 1, gs] + _dot_tn(b_g, xw[:, gs])
        y = jnp.concatenate(ys, axis=1)
        if gated:
            gz = (y1_ref[rows, :] + y) * _silu(z_ref[rows, :])
            ms = jnp.mean(gz * gz, axis=-1, keepdims=True)
            o_ref[rows, :] = (gz * lax.rsqrt(ms + EPS) * gn_ref[...]).astype(BF16)
        else:
            o_ref[rows, :] = dsk_ref[...] * xs + y


def _ssd(xs, bc, dt, avec, ex, dsk, n_ctx, reverse, y1=None, z=None, gn=None):
    b_sz, t, _ = xs.shape
    q = TM
    nc = t // q
    ncc = n_ctx // q
    if reverse:
        cmap = lambda b, s: (b, jnp.where(s < ncc, ncc - 1 - s, nc - 1 - (s - ncc)), 0)
    else:
        cmap = lambda b, s: (b, s, 0)
    row = lambda n: pl.BlockSpec((None, q, n), cmap)
    gated = y1 is not None
    args = [xs, bc, dt, avec, ex, dsk]
    specs = [row(D_INNER), row(bc.shape[-1]), row(LANES), _full(avec.shape), _full(ex.shape), _full(dsk.shape)]
    if gated:
        args += [y1, z, gn]
        specs += [row(D_INNER), row(D_INNER), _full(gn.shape)]
    return pl.pallas_call(
        functools.partial(_ssd_kernel, reverse=reverse, gated=gated),
        grid=(b_sz, nc),
        in_specs=specs,
        out_specs=row(D_INNER),
        out_shape=jax.ShapeDtypeStruct((b_sz, t, D_INNER), BF16 if gated else F32),
        scratch_shapes=[pltpu.VMEM((D_STATE, D_INNER), F32)],
        compiler_params=_cparams(("parallel", "arbitrary")),
        name="ssd_scan_bwd_gate" if gated else "ssd_scan_fwd",
    )(*args)


def _rope_tables(seq, n_ctx, head_dim):
    pos = jnp.arange(seq)
    row = (pos // GRID_W).astype(F32)
    col = (pos % GRID_W).astype(F32)
    quarter = head_dim // 4
    inv = ROPE_THETA ** (-jnp.arange(quarter, dtype=F32) / quarter)
    ar = row[:, None] * inv
    ac = col[:, None] * inv
    ang = jnp.concatenate([ar, ar, ac, ac], axis=-1)
    sign = jnp.where((jnp.arange(head_dim) % (head_dim // 2)) < quarter, -1.0, 1.0).astype(F32)
    cos = jnp.concatenate([jnp.ones((n_ctx, head_dim), F32), jnp.cos(ang)], axis=0)
    sin = jnp.concatenate([jnp.zeros((n_ctx, head_dim), F32), jnp.sin(ang) * sign], axis=0)
    rep = LANES // head_dim
    return jnp.tile(cos, (1, rep)), jnp.tile(sin, (1, rep))


def _dup_heads(w, head_dim):
    d, n = w.shape
    w = w.reshape(d, n // head_dim, 1, head_dim)
    return jnp.broadcast_to(w, (d, n // head_dim, 2, head_dim)).reshape(d, 2 * n)


def kernel(x, c, ctx, c_ctx, mod_w, mod_b, norm1_g, norm2_g, ffn_w_in, ffn_w_out, ev_w_in, ev_w_out, ev_q_g, ev_k_g, ev_dw_w, ev_dw_b, ev_cn_g, ev_cn_b, od_w_in, od_w_out, od_q_g, od_k_g, od_sink, od_conv_w, od_conv_b, od_dt_bias, od_A_log, od_D, od_gnorm_g):
    b_sz, seq, d = x.shape
    n_ctx = ctx.shape[1]
    depth = mod_w.shape[0]
    assert seq % TM == 0 and n_ctx % TM == 0 and seq >= TM + 2 * C_WINDOW and b_sz + 1 <= MOD_ROWS
    nctx_tiles = n_ctx // TM

    xs = (ctx, x)
    cc = jnp.zeros((MOD_ROWS, d), F32).at[:b_sz].set(c).at[b_sz].set(c_ctx)
    mod = _modulation(cc, mod_w, mod_b).reshape(depth, MOD_ROWS, 6, d)
    g1 = norm1_g.reshape(depth, 1, d)
    g2 = norm2_g.reshape(depth, 1, d)

    cos_a, sin_a = _rope_tables(seq, n_ctx, A_HEAD_DIM)
    cos_c, sin_c = _rope_tables(seq, n_ctx, C_HEAD_DIM)
    lane = jnp.arange(LANES)
    bd = (lane[:, None] // C_HEAD_DIM == lane[None, :] // C_HEAD_DIM).astype(BF16)
    bd = jnp.concatenate([bd, bd], axis=0)
    head_of_lane = jnp.arange(D_INNER) // D_HEAD_DIM
    ex = []
    for dirn in range(2):
        e = (lane[:, None] == head_of_lane[None, :] + dirn * D_HEADS).astype(BF16)
        ex.append(jnp.concatenate([e, e], axis=0))

    for l in range(depth):
        i = l // 2
        wi = ffn_w_in[l].astype(BF16)
        wo = ffn_w_out[l].astype(BF16)
        if l % 2 == 0:
            w_in = ev_w_in[i].astype(BF16)
            w_out = ev_w_out[i].astype(BF16)
            q, k, v, u = _ev_in(xs, mod, g1, w_in, ev_q_g[i].reshape(1, -1), ev_k_g[i].reshape(1, -1),
                                cos_a, sin_a, l, nctx_tiles)
            a1 = _ev_attn(q, k, v, n_ctx)
            a2 = _ev_conv(u, ev_dw_w[i], ev_dw_b[i].reshape(1, -1), ev_cn_g[i].reshape(1, -1),
                          ev_cn_b[i].reshape(1, -1), n_ctx)
            wo1, wo2 = w_out[:A_Q], w_out[A_Q:]
        else:
            w = od_w_in[i]
            o_k, o_v, o_x, o_dt, o_z = C_Q, C_Q + C_KV, C_Q + 2 * C_KV, C_Q + 2 * C_KV + D_XBC, C_Q + 2 * C_KV + D_XBC + 2 * D_HEADS
            w_in = jnp.concatenate([
                w[:, :o_k], _dup_heads(w[:, o_k:o_v], C_HEAD_DIM), _dup_heads(w[:, o_v:o_x], C_HEAD_DIM),
                w[:, o_x:o_dt], w[:, o_z:], w[:, o_dt:o_z], jnp.zeros((d, LANES - 2 * D_HEADS), F32)], axis=1).astype(BF16)
            w_out = od_w_out[i].astype(BF16)
            pad = jnp.zeros((LANES - 2 * D_HEADS,), F32)
            dtb = jnp.concatenate([od_dt_bias[i].reshape(-1), pad]).reshape(1, LANES)
            avec = jnp.concatenate([-jnp.exp(od_A_log[i].astype(F32)).reshape(-1), pad]).reshape(1, LANES)
            dsk = jnp.repeat(od_D[i].astype(F32), D_HEAD_DIM).reshape(1, D_INNER)
            qg = jnp.tile(od_q_g[i], LANES // C_HEAD_DIM).reshape(1, LANES)
            kg = jnp.tile(od_k_g[i], LANES // C_HEAD_DIM).reshape(1, LANES)
            q, k, v, xbc, z, dt = _od_in(xs, mod, g1, w_in, qg, kg, cos_c, sin_c, bd, dtb, l, nctx_tiles)
            a1 = _od_attn(od_sink[i], q, k, v, n_ctx)
            xc, bc = _od_conv(xbc, od_conv_w[i], od_conv_b[i].reshape(1, -1), n_ctx)
            y1 = _ssd(xc, bc, dt, avec, ex[0], dsk, n_ctx, reverse=False)
            a2 = _ssd(xc, bc, dt, avec, ex[1], dsk, n_ctx, reverse=True, y1=y1, z=z,
                      gn=od_gnorm_g[i].reshape(1, -1))
            wo1, wo2 = w_out[:C_Q], w_out[C_Q:]
        xs = _out_ffn(xs, a1, a2, mod, g2, wo1, wo2, wi, wo, l, nctx_tiles, latent_only=(l == depth - 1))
    return xs
```

```python
import functools
import math

import jax
import jax.numpy as jnp
from jax import lax
from jax.experimental import pallas as pl
from jax.experimental.pallas import tpu as pltpu

F32 = jnp.float32
BF16 = jnp.bfloat16

EPS = 1e-6
ROPE_THETA = 10000.0
GRID_W = 64
LOG2E = math.log2(math.e)
NEG = -1e30

A_HEADS, A_KV_HEADS, A_HEAD_DIM = 8, 2, 128
B_WIDTH, B_CONV = 512, 31
C_HEADS, C_KV_HEADS, C_HEAD_DIM, C_WINDOW = 16, 2, 64, 128
D_INNER, D_HEAD_DIM, D_HEADS, D_STATE, D_GROUPS, D_CONV, D_CHUNK = 1024, 64, 16, 128, 2, 5, 128
A_Q, A_KV = A_HEADS * A_HEAD_DIM, A_KV_HEADS * A_HEAD_DIM
C_Q, C_KV = C_HEADS * C_HEAD_DIM, C_KV_HEADS * C_HEAD_DIM
D_XBC = D_INNER + 2 * D_GROUPS * D_STATE

LANES = 128
SUBLANES = 8
VMEM_BYTES = 64 * 1024 * 1024
VMEM_LIMIT_RESIDENT = VMEM_BYTES * 7 // 8

TM = 256
MOD_ROWS = 16
EV_HALO = 16
OD_HALO = 8
EV_CONV_RB = 32


def _cparams(sem, resident=False):
    kw = dict(dimension_semantics=sem)
    if resident:
        kw["vmem_limit_bytes"] = VMEM_LIMIT_RESIDENT
    return pltpu.CompilerParams(**kw)


def _dot(a, b):
    return jnp.dot(a, b, preferred_element_type=F32)


def _dot_nt(a, b):
    return lax.dot_general(a, b, (((1,), (1,)), ((), ())), preferred_element_type=F32)


def _dot_tn(a, b):
    return lax.dot_general(a, b, (((0,), (0,)), ((), ())), preferred_element_type=F32)


def _silu(x):
    return x * jax.nn.sigmoid(x)


def _split2(x):
    hi = x.astype(BF16)
    lo = (x - hi.astype(F32)).astype(BF16)
    return hi, lo


def _prenorm(x, g, shift, scale):
    ms = jnp.mean(x * x, axis=-1, keepdims=True)
    return (x * lax.rsqrt(ms + EPS) * g) * (1.0 + scale) + shift


def _mod_kernel(c_ref, w_ref, b_ref, o_ref):
    a = _silu(c_ref[...]).astype(BF16)
    o_ref[...] = _dot(a, w_ref[...].astype(BF16)) + b_ref[...]


def _modulation(cc, mod_w, mod_b):
    depth, d, n = mod_w.shape
    tn = 6 * LANES * 2
    return pl.pallas_call(
        _mod_kernel,
        grid=(depth, n // tn),
        in_specs=[pl.BlockSpec((MOD_ROWS, d), lambda l, j: (0, 0)),
                  pl.BlockSpec((None, d, tn), lambda l, j: (l, 0, j)),
                  pl.BlockSpec((None, 1, tn), lambda l, j: (l, 0, j))],
        out_specs=pl.BlockSpec((None, MOD_ROWS, tn), lambda l, j: (l, 0, j)),
        out_shape=jax.ShapeDtypeStruct((depth, MOD_ROWS, n), F32),
        compiler_params=_cparams(("parallel", "parallel")),
        name="modulation",
    )(cc, mod_w, mod_b.reshape(depth, 1, n))


def _ev_in_kernel(*refs, n_src, nctx_tiles):
    mod_ref, g_ref, w_ref, qg_ref, kg_ref, cos_ref, sin_ref, q_ref, k_ref, vt_ref, u_ref = refs[n_src:]
    x = _load_stream(refs[:n_src], nctx_tiles)
    hb = _prenorm(x, g_ref[...], mod_ref[0:1, :], mod_ref[1:2, :]).astype(BF16)
    cos = cos_ref[...]
    sin = sin_ref[...]
    lane = lax.broadcasted_iota(jnp.int32, (1, LANES), 1)
    first = (lane % (A_HEAD_DIM // 2)) < (A_HEAD_DIM // 4)

    def norm_rope(t, g):
        y = t * lax.rsqrt(jnp.mean(t * t, axis=-1, keepdims=True) + EPS) * g
        rot = jnp.where(first, pltpu.roll(y, LANES - A_HEAD_DIM // 4, 1), pltpu.roll(y, A_HEAD_DIM // 4, 1))
        return y * cos + rot * sin

    qkv = _dot(hb, w_ref[:, 0:A_Q + 2 * A_KV])
    qg = qg_ref[...] * (A_HEAD_DIM ** -0.5 * LOG2E)
    for h in range(A_HEADS):
        sl = slice(h * LANES, (h + 1) * LANES)
        q_ref[:, sl] = norm_rope(qkv[:, sl], qg).astype(BF16)
    for h in range(A_KV_HEADS):
        sl = slice(h * LANES, (h + 1) * LANES)
        k_ref[:, sl] = norm_rope(qkv[:, A_Q + h * LANES:A_Q + (h + 1) * LANES], kg_ref[...]).astype(BF16)
    for h in range(A_KV_HEADS):
        vt_ref[h] = qkv[:, A_Q + A_KV + h * LANES:A_Q + A_KV + (h + 1) * LANES].T.astype(BF16)
    glu = _dot(hb, w_ref[:, A_Q + 2 * A_KV:])
    u_ref[...] = glu[:, :B_WIDTH] * jax.nn.sigmoid(glu[:, B_WIDTH:])


def _row_specs(b_sz, nctx_tiles, d, layer, first_tile=0):
    mod_spec = pl.BlockSpec((None, None, 6, d),
                            lambda b, i: (layer, jnp.where(i + first_tile < nctx_tiles, b_sz, b), 0, 0))
    g_spec = pl.BlockSpec((None, 1, d), lambda b, i: (layer, 0, 0))
    return mod_spec, g_spec


def _stream_specs(src, nctx_tiles, first_tile=0):
    if not isinstance(src, tuple):
        d = src.shape[-1]
        return [pl.BlockSpec((None, TM, d), lambda b, i: (b, jnp.maximum(i + first_tile, 0), 0))], [src], src.shape
    ctx, x = src
    d = x.shape[-1]
    c_spec = pl.BlockSpec((None, TM, d), lambda b, i: (b, jnp.clip(i + first_tile, 0, nctx_tiles - 1), 0))
    x_spec = pl.BlockSpec((None, TM, d), lambda b, i: (b, jnp.maximum(i + first_tile - nctx_tiles, 0), 0))
    return [c_spec, x_spec], [ctx, x], (x.shape[0], ctx.shape[1] + x.shape[1], d)


def _load_stream(src_refs, nctx_tiles, first_tile=0):
    if len(src_refs) == 1:
        return src_refs[0][...]
    return jnp.where(pl.program_id(1) + first_tile < nctx_tiles, src_refs[0][...], src_refs[1][...])


def _full(shape):
    nd = len(shape)
    return pl.BlockSpec(shape, lambda *_: (0,) * nd, pipeline_mode=pl.Buffered(1))


def _ev_in(src, mod, norm_g, w, qg, kg, cos, sin, layer, nctx_tiles):
    src_specs, src_arrays, (b_sz, t, d) = _stream_specs(src, nctx_tiles)
    mod_spec, g_spec = _row_specs(b_sz, nctx_tiles, d, layer)
    tab = pl.BlockSpec((TM, LANES), lambda b, i: (i, 0))
    out = lambda n: pl.BlockSpec((None, TM, n), lambda b, i: (b, i, 0))
    return pl.pallas_call(
        functools.partial(_ev_in_kernel, n_src=len(src_arrays), nctx_tiles=nctx_tiles),
        grid=(b_sz, t // TM),
        in_specs=src_specs + [mod_spec, g_spec, _full(w.shape), _full(qg.shape), _full(kg.shape), tab, tab],
        out_specs=[out(A_Q), out(A_KV),
                   pl.BlockSpec((None, A_KV_HEADS, None, A_HEAD_DIM, TM), lambda b, i: (b, 0, i, 0, 0)),
                   out(B_WIDTH)],
        out_shape=[jax.ShapeDtypeStruct((b_sz, t, A_Q), BF16),
                   jax.ShapeDtypeStruct((b_sz, t, A_KV), BF16),
                   jax.ShapeDtypeStruct((b_sz, A_KV_HEADS, t // TM, A_HEAD_DIM, TM), BF16),
                   jax.ShapeDtypeStruct((b_sz, t, B_WIDTH), F32)],
        compiler_params=_cparams(("parallel", "parallel"), resident=True),
        name="even_in_proj",
    )(*src_arrays, mod, norm_g, w, qg, kg, cos, sin)


def _ev_attn_kernel(q_ref, k_ref, vt_ref, o_ref, acc_ref, s_ref, *, kb_per_step, n_ctx, nctx_tiles):
    i = pl.program_id(2)
    tm = q_ref.shape[0]
    kb = vt_ref.shape[-1]
    grp = A_HEADS // A_KV_HEADS
    m_rows = grp * tm
    q = jnp.concatenate([q_ref[:, j * LANES:(j + 1) * LANES] for j in range(grp)], axis=0)

    def softmax_pv(s, vt, m, l, first):
        m_new = jnp.maximum(m, jnp.max(s, axis=0, keepdims=True))
        p = jnp.exp2(s - m_new)
        alpha = jnp.exp2(m - m_new)
        l_new = alpha * l + jnp.sum(p, axis=0, keepdims=True)
        pv = _dot(vt, p.astype(BF16))
        acc_ref[...] = pv if first else acc_ref[...] * alpha + pv
        return m_new, l_new

    def finish(l):
        o = (acc_ref[...] / l).T
        for j in range(grp):
            o_ref[:, j * LANES:(j + 1) * LANES] = o[j * tm:(j + 1) * tm].astype(BF16)

    nb_ctx = n_ctx // kb

    def ctx_block(s_ctx):
        vt = jnp.concatenate([vt_ref[j] for j in range(nb_ctx)], axis=1)
        return softmax_pv(s_ctx, vt, jnp.full((1, m_rows), NEG, F32), jnp.zeros((1, m_rows), F32), True)

    @pl.when(i < nctx_tiles)
    def _():
        _, l_ctx = ctx_block(_dot_nt(k_ref[0:n_ctx, :], q))
        finish(l_ctx)

    kps = kb_per_step

    def scores(blk, slot):
        off = pl.multiple_of(blk * kb, kb)
        s_ref[slot] = _dot_nt(k_ref[pl.ds(off, kps * kb), :], q)

    def consume(blk, slot, carry):
        vt = jnp.concatenate([vt_ref[blk + r] for r in range(kps)], axis=1)
        return softmax_pv(s_ref[slot], vt, *carry, False)

    n_pairs = (vt_ref.shape[0] - nb_ctx) // (2 * kps)

    @pl.when(i >= nctx_tiles)
    def _():
        s_ctx = _dot_nt(k_ref[0:n_ctx, :], q)
        scores(nb_ctx, 0)
        m, l = ctx_block(s_ctx)

        def body(jj, carry):
            blk = nb_ctx + 2 * kps * jj
            scores(blk + kps, 1)
            carry = consume(blk, 0, carry)
            scores(blk + 2 * kps, 0)
            return consume(blk + kps, 1, carry)

        carry = lax.fori_loop(0, n_pairs - 1, body, (m, l))
        blk = nb_ctx + 2 * kps * (n_pairs - 1)
        scores(blk + kps, 1)
        carry = consume(blk, 0, carry)
        _, l_fin = consume(blk + kps, 1, carry)
        finish(l_fin)


def _ev_attn(q, k, vt, n_ctx):
    b_sz, t, _ = q.shape
    nblk, kb = vt.shape[2], vt.shape[4]
    kb_per_step = 4
    assert (nblk - n_ctx // kb) % (2 * kb_per_step) == 0
    grp = A_HEADS // A_KV_HEADS
    grp_w = grp * A_HEAD_DIM
    kern = functools.partial(_ev_attn_kernel, kb_per_step=kb_per_step, n_ctx=n_ctx, nctx_tiles=n_ctx // TM)
    return pl.pallas_call(
        kern,
        grid=(b_sz, A_KV_HEADS, t // TM),
        in_specs=[pl.BlockSpec((None, TM, grp_w), lambda b, g, i: (b, i, g)),
                  pl.BlockSpec((None, t, A_HEAD_DIM), lambda b, g, i: (b, 0, g)),
                  pl.BlockSpec((None, None, nblk, A_HEAD_DIM, kb), lambda b, g, i: (b, g, 0, 0, 0))],
        out_specs=pl.BlockSpec((None, TM, grp_w), lambda b, g, i: (b, i, g)),
        out_shape=jax.ShapeDtypeStruct((b_sz, t, A_Q), BF16),
        scratch_shapes=[pltpu.VMEM((A_HEAD_DIM, grp * TM), F32),
                        pltpu.VMEM((2, kb_per_step * kb, grp * TM), F32)],
        compiler_params=_cparams(("parallel", "parallel", "arbitrary"), resident=True),
        name="even_attention",
    )(q, k, vt)


def _halo_flags(i, nctx_tiles, n_tiles):
    has_prev = jnp.logical_and(i != 0, i != nctx_tiles)
    has_next = jnp.logical_and(i != nctx_tiles - 1, i != n_tiles - 1)
    return has_prev, has_next


def _conv_module_tile(u_ref, up_ref, un_ref, has_prev, has_next, w_ref, b_ref, g_ref, bb_ref, buf, sh, rb, emit):
    tm = u_ref.shape[0]
    buf[0:EV_HALO, :] = jnp.where(has_prev, up_ref[...], 0.0)
    buf[EV_HALO:EV_HALO + tm, :] = u_ref[...]
    buf[EV_HALO + tm:, :] = jnp.where(has_next, un_ref[...], 0.0)
    for s in range(1, SUBLANES):
        sh[s - 1] = buf[pl.ds(s, sh.shape[1]), :]
    base = EV_HALO - B_CONV // 2
    for r in range(tm // rb):
        acc = jnp.zeros((rb, B_WIDTH), F32) + b_ref[...]
        for k in range(B_CONV):
            s = (base + k) % SUBLANES
            al = r * rb + base + k - s
            src = buf[al:al + rb, :] if s == 0 else sh[s - 1, al:al + rb, :]
            acc = acc + src * w_ref[k:k + 1, :]
        mu = jnp.mean(acc, axis=-1, keepdims=True)
        cen = acc - mu
        var = jnp.mean(cen * cen, axis=-1, keepdims=True)
        y = cen * lax.rsqrt(var + EPS) * g_ref[...] + bb_ref[...]
        emit(slice(r * rb, (r + 1) * rb), _silu(y).astype(BF16))


def _ev_conv_kernel(u_ref, up_ref, un_ref, w_ref, b_ref, g_ref, bb_ref, o_ref, buf, sh, *, nctx_tiles, n_tiles, rb):
    has_prev, has_next = _halo_flags(pl.program_id(1), nctx_tiles, n_tiles)

    def emit(rows, v):
        o_ref[rows, :] = v

    _conv_module_tile(u_ref, up_ref, un_ref, has_prev, has_next, w_ref, b_ref, g_ref, bb_ref, buf, sh, rb, emit)


def _halo_specs(t, width, halo):
    per = TM // halo
    last = t // halo - 1
    main = pl.BlockSpec((None, TM, width), lambda b, i: (b, i, 0))
    prev = pl.BlockSpec((None, halo, width), lambda b, i: (b, jnp.maximum(i * per - 1, 0), 0))
    nxt = pl.BlockSpec((None, halo, width), lambda b, i: (b, jnp.minimum((i + 1) * per, last), 0))
    return [main, prev, nxt]


def _ev_conv(u, w, bias, g, bb, n_ctx):
    b_sz, t, c = u.shape
    n_tiles = t // TM
    kern = functools.partial(_ev_conv_kernel, nctx_tiles=n_ctx // TM, n_tiles=n_tiles, rb=EV_CONV_RB)
    return pl.pallas_call(
        kern,
        grid=(b_sz, n_tiles),
        in_specs=_halo_specs(t, c, EV_HALO) + [_full(w.shape), _full(bias.shape), _full(g.shape), _full(bb.shape)],
        out_specs=pl.BlockSpec((None, TM, c), lambda b, i: (b, i, 0)),
        out_shape=jax.ShapeDtypeStruct((b_sz, t, c), BF16),
        scratch_shapes=[pltpu.VMEM((TM + 2 * EV_HALO, c), F32),
                        pltpu.VMEM((SUBLANES - 1, TM + 2 * EV_HALO - SUBLANES, c), F32)],
        compiler_params=_cparams(("parallel", "parallel")),
        name="even_conv_module",
    )(u, u, u, w, bias, g, bb)


def _ffn_tile(x, a1, a2, mod_ref, g_ref, wo1_ref, wo2_ref, wi_ref, wo_ref, act_ref, hid, chunk):
    o = _dot(a1, wo1_ref[...]) + _dot(a2, wo2_ref[...])
    x1 = x + mod_ref[2:3, :] * o
    h2 = _prenorm(x1, g_ref[...], mod_ref[3:4, :], mod_ref[4:5, :]).astype(BF16)
    for c in range(hid // chunk):
        gte = _dot(h2, wi_ref[:, c * chunk:(c + 1) * chunk])
        up = _dot(h2, wi_ref[:, hid + c * chunk:hid + (c + 1) * chunk])
        act_ref[:, c * chunk:(c + 1) * chunk] = (_silu(gte) * up).astype(BF16)
    return x1 + mod_ref[5:6, :] * _dot(act_ref[...], wo_ref[...])


def _out_ffn_kernel(*refs, n_src, nctx_tiles, first_tile, hid, chunk):
    a1_ref, a2_ref, mod_ref, g_ref, wo1_ref, wo2_ref, wi_ref, wo_ref, o_ref, act_ref = refs[n_src:]
    x = _load_stream(refs[:n_src], nctx_tiles, first_tile)
    o_ref[...] = _ffn_tile(x, a1_ref[...], a2_ref[...], mod_ref, g_ref, wo1_ref, wo2_ref, wi_ref, wo_ref,
                           act_ref, hid, chunk)


def _out_ffn(src, a1, a2, mod, norm_g, wo1, wo2, wi, wo, layer, nctx_tiles, latent_only):
    first = nctx_tiles if latent_only else 0
    src_specs, src_arrays, (b_sz, t, d) = _stream_specs(src, nctx_tiles, first)
    hid = wo.shape[0]
    n_tiles = t // TM - first
    mod_spec, g_spec = _row_specs(b_sz, nctx_tiles, d, layer, first)
    row = lambda n: pl.BlockSpec((None, TM, n), lambda b, i: (b, i + first, 0))
    kern = functools.partial(_out_ffn_kernel, n_src=len(src_arrays), nctx_tiles=nctx_tiles, first_tile=first,
                             hid=hid, chunk=2 * LANES)
    in_place = len(src_arrays) == 1 and not latent_only
    return pl.pallas_call(
        kern,
        grid=(b_sz, n_tiles),
        in_specs=src_specs + [row(a1.shape[-1]), row(a2.shape[-1]), mod_spec, g_spec,
                              _full(wo1.shape), _full(wo2.shape), _full(wi.shape), _full(wo.shape)],
        out_specs=pl.BlockSpec((None, TM, d), lambda b, i: (b, i, 0)),
        out_shape=jax.ShapeDtypeStruct((b_sz, n_tiles * TM, d), F32),
        scratch_shapes=[pltpu.VMEM((TM, hid), BF16)],
        input_output_aliases={0: 0} if in_place else {},
        compiler_params=_cparams(("parallel", "parallel"), resident=True),
        name="out_proj_ffn",
    )(*src_arrays, a1, a2, mod, norm_g, wo1, wo2, wi, wo)


def _od_in_kernel(x_ref, mod_ref, g_ref, w_ref, qg_ref, kg_ref, cos_ref, sin_ref, bd_ref, dtb_ref,
                  q_ref, k_ref, vt_ref, xbc_ref, z_ref, dt_ref):
    hb = _prenorm(x_ref[...], g_ref[...], mod_ref[0:1, :], mod_ref[1:2, :]).astype(BF16)
    cos = cos_ref[...]
    sin = sin_ref[...]
    bd = bd_ref[...]
    lane = lax.broadcasted_iota(jnp.int32, (1, LANES), 1)
    first = (lane % (C_HEAD_DIM // 2)) < (C_HEAD_DIM // 4)

    def norm_rope(t, g):
        hi, lo = _split2(t * t)
        ss = _dot(jnp.concatenate([hi, lo], axis=1), bd)
        y = t * lax.rsqrt(ss * (1.0 / C_HEAD_DIM) + EPS) * g
        rot = jnp.where(first, pltpu.roll(y, LANES - C_HEAD_DIM // 4, 1), pltpu.roll(y, C_HEAD_DIM // 4, 1))
        return y * cos + rot * sin

    nq = C_Q // LANES
    nk = 2 * C_KV // LANES
    qkv = _dot(hb, w_ref[:, 0:C_Q + 4 * C_KV])
    qg = qg_ref[...] * (C_HEAD_DIM ** -0.5 * LOG2E)
    for j in range(nq):
        sl = slice(j * LANES, (j + 1) * LANES)
        q_ref[:, sl] = norm_rope(qkv[:, sl], qg).astype(BF16)
    for j in range(nk):
        k_ref[:, j * LANES:(j + 1) * LANES] = norm_rope(qkv[:, C_Q + j * LANES:C_Q + (j + 1) * LANES], kg_ref[...]).astype(BF16)
    for j in range(nk):
        vt_ref[j] = qkv[:, C_Q + 2 * C_KV + j * LANES:C_Q + 2 * C_KV + (j + 1) * LANES].T.astype(BF16)
    rest = _dot(hb, w_ref[:, C_Q + 4 * C_KV:])
    xbc_ref[...] = rest[:, 0:D_XBC]
    z_ref[...] = rest[:, D_XBC:D_XBC + D_INNER]
    dtr = rest[:, D_XBC + D_INNER:] + dtb_ref[...]
    dt_ref[...] = jnp.maximum(dtr, 0.0) + jnp.log1p(jnp.exp(-jnp.abs(dtr)))


def _od_in(xs, mod, norm_g, w, qg, kg, cos, sin, bd, dtb, layer, nctx_tiles):
    b_sz, t, d = xs.shape
    x_spec = pl.BlockSpec((None, TM, d), lambda b, i: (b, i, 0))
    mod_spec, g_spec = _row_specs(b_sz, nctx_tiles, d, layer)
    tab = pl.BlockSpec((TM, LANES), lambda b, i: (i, 0))
    out = lambda n: pl.BlockSpec((None, TM, n), lambda b, i: (b, i, 0))
    return pl.pallas_call(
        _od_in_kernel,
        grid=(b_sz, t // TM),
        in_specs=[x_spec, mod_spec, g_spec, _full(w.shape), _full(qg.shape), _full(kg.shape), tab, tab,
                  _full(bd.shape), _full(dtb.shape)],
        out_specs=[out(C_Q), out(2 * C_KV),
                   pl.BlockSpec((None, C_KV_HEADS, None, LANES, TM), lambda b, i: (b, 0, i, 0, 0)),
                   out(D_XBC), out(D_INNER), out(LANES)],
        out_shape=[jax.ShapeDtypeStruct((b_sz, t, C_Q), BF16),
                   jax.ShapeDtypeStruct((b_sz, t, 2 * C_KV), BF16),
                   jax.ShapeDtypeStruct((b_sz, C_KV_HEADS, t // TM, LANES, TM), BF16),
                   jax.ShapeDtypeStruct((b_sz, t, D_XBC), F32),
                   jax.ShapeDtypeStruct((b_sz, t, D_INNER), F32),
                   jax.ShapeDtypeStruct((b_sz, t, LANES), F32)],
        compiler_params=_cparams(("parallel", "parallel"), resident=True),
        name="odd_in_proj",
    )(xs, mod, norm_g, w, qg, kg, cos, sin, bd, dtb)


def _od_attn_kernel(sink_ref, q_ref, k_ref, vt_ref, o_ref, s_ref, *, n_ctx, t, nctx_tiles):
    i = pl.program_id(1)
    tm = q_ref.shape[0]
    w = C_WINDOW
    n_tiles = vt_ref.shape[1]
    grp = C_HEADS // C_KV_HEADS
    span = tm + 2 * w
    nkeys = n_ctx + span
    prev = jnp.maximum(i - 1, 0)
    nxt = jnp.minimum(i + 1, n_tiles - 1)
    zero = jnp.zeros((), BF16)
    low_lane = lax.broadcasted_iota(jnp.int32, (1, LANES), 1) < C_HEAD_DIM
    low_row = lax.broadcasted_iota(jnp.int32, (LANES, 1), 0) < C_HEAD_DIM
    k2, vt2 = [], []
    for g in range(C_KV_HEADS):
        gl = slice(g * LANES, (g + 1) * LANES)
        k_all = jnp.concatenate([k_ref[0:n_ctx, gl],
                                 k_ref[pl.ds(pl.multiple_of(prev * tm + w, w), w), gl],
                                 k_ref[pl.ds(pl.multiple_of(i * tm, tm), tm), gl],
                                 k_ref[pl.ds(pl.multiple_of(nxt * tm, tm), w), gl]], axis=0)
        vt_all = jnp.concatenate([vt_ref[g, j] for j in range(n_ctx // tm)]
                                 + [vt_ref[g, prev][:, w:], vt_ref[g, i], vt_ref[g, nxt][:, :w]], axis=1)
        k2.append(jnp.concatenate([jnp.where(low_lane, k_all, zero), jnp.where(low_lane, zero, k_all)], axis=0))
        vt2.append(jnp.concatenate([jnp.where(low_row, vt_all, zero), jnp.where(low_row, zero, vt_all)], axis=1))
    rr = lax.broadcasted_iota(jnp.int32, (span, tm), 0)
    cc = lax.broadcasted_iota(jnp.int32, (span, tm), 1)
    kpos = i * tm - w + rr
    ok = jnp.logical_and(jnp.abs(rr - w - cc) <= w, jnp.logical_and(kpos >= n_ctx, kpos < t))
    bias = jnp.where(jnp.logical_and(ok, i >= nctx_tiles), 0.0, NEG)
    n_pairs = C_HEADS // 2

    def scores(n):
        s_ref[n % 2] = _dot_nt(k2[n // (grp // 2)], q_ref[:, n * LANES:(n + 1) * LANES])

    scores(0)
    for n in range(n_pairs):
        if n + 1 < n_pairs:
            scores(n + 1)
        ps, invs = [], []
        for par in range(2):
            sk = sink_ref[2 * n + par] * LOG2E
            s_c = s_ref[n % 2, par * nkeys:par * nkeys + n_ctx, :]
            s_l = s_ref[n % 2, par * nkeys + n_ctx:(par + 1) * nkeys, :] + bias
            m = jnp.maximum(jnp.maximum(jnp.max(s_c, axis=0, keepdims=True), jnp.max(s_l, axis=0, keepdims=True)), sk)
            p_c = jnp.exp2(s_c - m)
            p_l = jnp.exp2(s_l - m)
            den = jnp.sum(p_c, axis=0, keepdims=True) + jnp.sum(p_l, axis=0, keepdims=True) + jnp.exp2(sk - m)
            ps += [p_c.astype(BF16), p_l.astype(BF16)]
            invs.append(1.0 / den)
        o_t = _dot(vt2[n // (grp // 2)], jnp.concatenate(ps, axis=0)) * jnp.where(low_row, invs[0], invs[1])
        o_ref[:, n * LANES:(n + 1) * LANES] = o_t.T.astype(BF16)


def _od_attn(sink, q, k, vt, n_ctx):
    b_sz, t, _ = q.shape
    assert TM == 2 * C_WINDOW
    kern = functools.partial(_od_attn_kernel, n_ctx=n_ctx, t=t, nctx_tiles=n_ctx // TM)
    return pl.pallas_call(
        kern,
        grid=(b_sz, t // TM),
        in_specs=[pl.BlockSpec(memory_space=pltpu.SMEM),
                  pl.BlockSpec((None, TM, C_Q), lambda b, i: (b, i, 0)),
                  pl.BlockSpec((None, t, 2 * C_KV), lambda b, i: (b, 0, 0)),
                  pl.BlockSpec((None, C_KV_HEADS, t // TM, LANES, TM), lambda b, i: (b, 0, 0, 0, 0))],
        out_specs=pl.BlockSpec((None, TM, C_Q), lambda b, i: (b, i, 0)),
        out_shape=jax.ShapeDtypeStruct((b_sz, t, C_Q), BF16),
        scratch_shapes=[pltpu.VMEM((2, 2 * (n_ctx + TM + 2 * C_WINDOW), TM), F32)],
        compiler_params=_cparams(("parallel", "arbitrary"), resident=True),
        name="odd_attention",
    )(sink, q, k, vt)


def _od_conv_kernel(u_ref, up_ref, un_ref, w_ref, b_ref, xs_ref, bc_ref, buf, *, nctx_tiles, n_tiles, rb):
    i = pl.program_id(1)
    has_prev, has_next = _halo_flags(i, nctx_tiles, n_tiles)
    tm = u_ref.shape[0]
    buf[0:OD_HALO, :] = jnp.where(has_prev, up_ref[...], 0.0)
    buf[OD_HALO:OD_HALO + tm, :] = u_ref[...]
    buf[OD_HALO + tm:, :] = jnp.where(has_next, un_ref[...], 0.0)
    base = OD_HALO - D_CONV // 2
    cw = bc_ref.shape[1]
    for r in range(tm // rb):
        rows = slice(r * rb, (r + 1) * rb)
        for cb in range(D_XBC // cw):
            cs = slice(cb * cw, (cb + 1) * cw)
            acc = jnp.zeros((rb, cw), F32) + b_ref[:, cs]
            for k in range(D_CONV):
                acc = acc + buf[pl.ds(r * rb + base + k, rb), cs] * w_ref[k:k + 1, cs]
            y = _silu(acc)
            if cb * cw < D_INNER:
                xs_ref[rows, cs] = y
            else:
                bc_ref[rows, :] = y.astype(BF16)


def _od_conv(xbc, w, bias, n_ctx):
    b_sz, t, c = xbc.shape
    n_tiles = t // TM
    kern = functools.partial(_od_conv_kernel, nctx_tiles=n_ctx // TM, n_tiles=n_tiles, rb=64)
    out = lambda n: pl.BlockSpec((None, TM, n), lambda b, i: (b, i, 0))
    return pl.pallas_call(
        kern,
        grid=(b_sz, n_tiles),
        in_specs=_halo_specs(t, c, OD_HALO) + [_full(w.shape), _full(bias.shape)],
        out_specs=[out(D_INNER), out(c - D_INNER)],
        out_shape=[jax.ShapeDtypeStruct((b_sz, t, D_INNER), F32),
                   jax.ShapeDtypeStruct((b_sz, t, c - D_INNER), BF16)],
        scratch_shapes=[pltpu.VMEM((TM + 2 * OD_HALO, c), F32)],
        compiler_params=_cparams(("parallel", "parallel")),
        name="odd_ssd_conv",
    )(xbc, xbc, xbc, w, bias)


def _ssd_kernel(*refs, reverse, gated):
    if gated:
        xs_ref, bc_ref, dt_ref, av_ref, ex_ref, dsk_ref, y1_ref, z_ref, gn_ref, o_ref, h_ref = refs
    else:
        xs_ref, bc_ref, dt_ref, av_ref, ex_ref, dsk_ref, o_ref, h_ref = refs
    q = D_CHUNK
    gw = D_INNER // D_GROUPS
    hpg = D_HEADS // D_GROUPS
    col0 = D_HEADS if reverse else 0

    @pl.when(pl.program_id(1) == 0)
    def _():
        h_ref[...] = jnp.zeros_like(h_ref)

    ii = lax.broadcasted_iota(jnp.int32, (q, q), 0)
    jj = lax.broadcasted_iota(jnp.int32, (q, q), 1)
    tri = (jj >= ii) if reverse else (jj <= ii)
    tri_b = jnp.where(tri, 1.0, 0.0).astype(BF16)
    low = lax.broadcasted_iota(jnp.int32, (1, LANES), 1) < D_HEAD_DIM
    edge = 0 if reverse else q - 1

    def expand(w):
        hi, lo = _split2(w)
        return _dot(jnp.concatenate([hi, lo], axis=1), ex_ref[...])

    n_chunks = xs_ref.shape[0] // q
    for ch in (range(n_chunks - 1, -1, -1) if reverse else range(n_chunks)):
        rows = slice(ch * q, (ch + 1) * q)
        dt = dt_ref[rows, :]
        a = dt * av_ref[...]
        a1 = a.astype(BF16)
        r1 = a - a1.astype(F32)
        a2 = r1.astype(BF16)
        a3 = (r1 - a2.astype(F32)).astype(BF16)
        cs = _dot(tri_b, jnp.concatenate([a1, a2, a3], axis=1))
        acs = cs[:, 0:LANES] + cs[:, LANES:2 * LANES] + cs[:, 2 * LANES:3 * LANES]
        tot = acs[edge:edge + 1, :]
        xs = xs_ref[rows, :]
        w_y = expand(jnp.exp(acs))
        xdt = (xs * expand(dt)).astype(BF16)
        xw = (xs * expand(dt * jnp.exp(tot - acs))).astype(BF16)
        acs_t = acs.T
        bc = bc_ref[rows, :]
        h_in = h_ref[...]
        hb = h_in.astype(BF16)
        ys = []
        for g in range(D_GROUPS):
            gs = slice(g * gw, (g + 1) * gw)
            b_g = bc[:, g * D_STATE:(g + 1) * D_STATE]
            c_g = bc[:, (D_GROUPS + g) * D_STATE:(D_GROUPS + g + 1) * D_STATE]
            cb = _dot_nt(c_g, b_g)
            inter = _dot(c_g, hb[:, gs]) * w_y[:, gs]
            for pr in range(hpg // 2):
                x2 = xdt[:, g * gw + pr * LANES:g * gw + (pr + 1) * LANES]
                outs = []
                for par in range(2):
                    c = col0 + g * hpg + 2 * pr + par
                    seg = acs[:, c:c + 1] - acs_t[c:c + 1, :]
                    lmat = (jnp.exp(jnp.where(tri, seg, NEG)) * cb).astype(BF16)
                    outs.append(_dot(lmat, x2))
                ys.append(jnp.where(low, outs[0], outs[1]) + inter[:, pr * LANES:(pr + 1) * LANES])
            h_ref[:, gs] = h_in[:, gs] * w_y[edge:edge + 1, gs] + _dot_tn(b_g, xw[:, gs])
        y = jnp.concatenate(ys, axis=1)
        if gated:
            gz = (y1_ref[rows, :] + y) * _silu(z_ref[rows, :])
            ms = jnp.mean(gz * gz, axis=-1, keepdims=True)
            o_ref[rows, :] = (gz * lax.rsqrt(ms + EPS) * gn_ref[...]).astype(BF16)
        else:
            o_ref[rows, :] = dsk_ref[...] * xs + y


def _ssd(xs, bc, dt, avec, ex, dsk, n_ctx, reverse, y1=None, z=None, gn=None):
    b_sz, t, _ = xs.shape
    q = TM
    nc = t // q
    ncc = n_ctx // q
    if reverse:
        cmap = lambda b, s: (b, jnp.where(s < ncc, ncc - 1 - s, nc - 1 - (s - ncc)), 0)
    else:
        cmap = lambda b, s: (b, s, 0)
    row = lambda n: pl.BlockSpec((None, q, n), cmap)
    gated = y1 is not None
    args = [xs, bc, dt, avec, ex, dsk]
    specs = [row(D_INNER), row(bc.shape[-1]), row(LANES), _full(avec.shape), _full(ex.shape), _full(dsk.shape)]
    if gated:
        args += [y1, z, gn]
        specs += [row(D_INNER), row(D_INNER), _full(gn.shape)]
    return pl.pallas_call(
        functools.partial(_ssd_kernel, reverse=reverse, gated=gated),
        grid=(b_sz, nc),
        in_specs=specs,
        out_specs=row(D_INNER),
        out_shape=jax.ShapeDtypeStruct((b_sz, t, D_INNER), BF16 if gated else F32),
        scratch_shapes=[pltpu.VMEM((D_STATE, D_INNER), F32)],
        compiler_params=_cparams(("parallel", "arbitrary")),
        name="ssd_scan_bwd_gate" if gated else "ssd_scan_fwd",
    )(*args)


def _rope_tables(seq, n_ctx, head_dim):
    pos = jnp.arange(seq)
    row = (pos // GRID_W).astype(F32)
    col = (pos % GRID_W).astype(F32)
    quarter = head_dim // 4
    inv = ROPE_THETA ** (-jnp.arange(quarter, dtype=F32) / quarter)
    ar = row[:, None] * inv
    ac = col[:, None] * inv
    ang = jnp.concatenate([ar, ar, ac, ac], axis=-1)
    sign = jnp.where((jnp.arange(head_dim) % (head_dim // 2)) < quarter, -1.0, 1.0).astype(F32)
    cos = jnp.concatenate([jnp.ones((n_ctx, head_dim), F32), jnp.cos(ang)], axis=0)
    sin = jnp.concatenate([jnp.zeros((n_ctx, head_dim), F32), jnp.sin(ang) * sign], axis=0)
    rep = LANES // head_dim
    return jnp.tile(cos, (1, rep)), jnp.tile(sin, (1, rep))


def _dup_heads(w, head_dim):
    d, n = w.shape
    w = w.reshape(d, n // head_dim, 1, head_dim)
    return jnp.broadcast_to(w, (d, n // head_dim, 2, head_dim)).reshape(d, 2 * n)


def kernel(x, c, ctx, c_ctx, mod_w, mod_b, norm1_g, norm2_g, ffn_w_in, ffn_w_out, ev_w_in, ev_w_out, ev_q_g, ev_k_g, ev_dw_w, ev_dw_b, ev_cn_g, ev_cn_b, od_w_in, od_w_out, od_q_g, od_k_g, od_sink, od_conv_w, od_conv_b, od_dt_bias, od_A_log, od_D, od_gnorm_g):
    b_sz, seq, d = x.shape
    n_ctx = ctx.shape[1]
    depth = mod_w.shape[0]
    assert seq % TM == 0 and n_ctx % TM == 0 and seq >= TM + 2 * C_WINDOW and b_sz + 1 <= MOD_ROWS
    nctx_tiles = n_ctx // TM

    xs = (ctx, x)
    cc = jnp.zeros((MOD_ROWS, d), F32).at[:b_sz].set(c).at[b_sz].set(c_ctx)
    mod = _modulation(cc, mod_w, mod_b).reshape(depth, MOD_ROWS, 6, d)
    g1 = norm1_g.reshape(depth, 1, d)
    g2 = norm2_g.reshape(depth, 1, d)

    cos_a, sin_a = _rope_tables(seq, n_ctx, A_HEAD_DIM)
    cos_c, sin_c = _rope_tables(seq, n_ctx, C_HEAD_DIM)
    lane = jnp.arange(LANES)
    bd = (lane[:, None] // C_HEAD_DIM == lane[None, :] // C_HEAD_DIM).astype(BF16)
    bd = jnp.concatenate([bd, bd], axis=0)
    head_of_lane = jnp.arange(D_INNER) // D_HEAD_DIM
    ex = []
    for dirn in range(2):
        e = (lane[:, None] == head_of_lane[None, :] + dirn * D_HEADS).astype(BF16)
        ex.append(jnp.concatenate([e, e], axis=0))

    for l in range(depth):
        i = l // 2
        wi = ffn_w_in[l].astype(BF16)
        wo = ffn_w_out[l].astype(BF16)
        if l % 2 == 0:
            w_in = ev_w_in[i].astype(BF16)
            w_out = ev_w_out[i].astype(BF16)
            q, k, v, u = _ev_in(xs, mod, g1, w_in, ev_q_g[i].reshape(1, -1), ev_k_g[i].reshape(1, -1),
                                cos_a, sin_a, l, nctx_tiles)
            a1 = _ev_attn(q, k, v, n_ctx)
            conv_p = (ev_dw_w[i], ev_dw_b[i].reshape(1, -1), ev_cn_g[i].reshape(1, -1), ev_cn_b[i].reshape(1, -1))
            wo1, wo2 = w_out[:A_Q], w_out[A_Q:]
            a2 = _ev_conv(u, *conv_p, n_ctx)
        else:
            w = od_w_in[i]
            o_k, o_v, o_x, o_dt, o_z = C_Q, C_Q + C_KV, C_Q + 2 * C_KV, C_Q + 2 * C_KV + D_XBC, C_Q + 2 * C_KV + D_XBC + 2 * D_HEADS
            w_in = jnp.concatenate([
                w[:, :o_k], _dup_heads(w[:, o_k:o_v], C_HEAD_DIM), _dup_heads(w[:, o_v:o_x], C_HEAD_DIM),
                w[:, o_x:o_dt], w[:, o_z:], w[:, o_dt:o_z], jnp.zeros((d, LANES - 2 * D_HEADS), F32)], axis=1).astype(BF16)
            w_out = od_w_out[i].astype(BF16)
            pad = jnp.zeros((LANES - 2 * D_HEADS,), F32)
            dtb = jnp.concatenate([od_dt_bias[i].reshape(-1), pad]).reshape(1, LANES)
            avec = jnp.concatenate([-jnp.exp(od_A_log[i].astype(F32)).reshape(-1), pad]).reshape(1, LANES)
            dsk = jnp.repeat(od_D[i].astype(F32), D_HEAD_DIM).reshape(1, D_INNER)
            qg = jnp.tile(od_q_g[i], LANES // C_HEAD_DIM).reshape(1, LANES)
            kg = jnp.tile(od_k_g[i], LANES // C_HEAD_DIM).reshape(1, LANES)
            q, k, v, xbc, z, dt = _od_in(xs, mod, g1, w_in, qg, kg, cos_c, sin_c, bd, dtb, l, nctx_tiles)
            a1 = _od_attn(od_sink[i], q, k, v, n_ctx)
            xc, bc = _od_conv(xbc, od_conv_w[i], od_conv_b[i].reshape(1, -1), n_ctx)
            y1 = _ssd(xc, bc, dt, avec, ex[0], dsk, n_ctx, reverse=False)
            a2 = _ssd(xc, bc, dt, avec, ex[1], dsk, n_ctx, reverse=True, y1=y1, z=z,
                      gn=od_gnorm_g[i].reshape(1, -1))
            wo1, wo2 = w_out[:C_Q], w_out[C_Q:]
        xs = _out_ffn(xs, a1, a2, mod, g2, wo1, wo2, wi, wo, l, nctx_tiles, latent_only=(l == depth - 1))
    return xs
```

```python
import functools
import math

import jax
import jax.numpy as jnp
from jax import lax
from jax.experimental import pallas as pl
from jax.experimental.pallas import tpu as pltpu

F32 = jnp.float32
BF16 = jnp.bfloat16

EPS = 1e-6
ROPE_THETA = 10000.0
GRID_W = 64
LOG2E = math.log2(math.e)
NEG = -1e30

A_HEADS, A_KV_HEADS, A_HEAD_DIM = 8, 2, 128
B_WIDTH, B_CONV = 512, 31
C_HEADS, C_KV_HEADS, C_HEAD_DIM, C_WINDOW = 16, 2, 64, 128
D_INNER, D_HEAD_DIM, D_HEADS, D_STATE, D_GROUPS, D_CONV, D_CHUNK = 1024, 64, 16, 128, 2, 5, 128
A_Q, A_KV = A_HEADS * A_HEAD_DIM, A_KV_HEADS * A_HEAD_DIM
C_Q, C_KV = C_HEADS * C_HEAD_DIM, C_KV_HEADS * C_HEAD_DIM
D_XBC = D_INNER + 2 * D_GROUPS * D_STATE

LANES = 128
SUBLANES = 8
VMEM_BYTES = 64 * 1024 * 1024
VMEM_LIMIT_RESIDENT = VMEM_BYTES * 7 // 8

TM = 256
MOD_ROWS = 16
EV_HALO = 16
OD_HALO = 8
EV_CONV_RB = TM


def _cparams(sem, resident=False):
    kw = dict(dimension_semantics=sem)
    if resident:
        kw["vmem_limit_bytes"] = VMEM_LIMIT_RESIDENT
    return pltpu.CompilerParams(**kw)


def _dot(a, b):
    return jnp.dot(a, b, preferred_element_type=F32)


def _dot_nt(a, b):
    return lax.dot_general(a, b, (((1,), (1,)), ((), ())), preferred_element_type=F32)


def _dot_tn(a, b):
    return lax.dot_general(a, b, (((0,), (0,)), ((), ())), preferred_element_type=F32)


def _silu(x):
    return x * jax.nn.sigmoid(x)


def _split2(x):
    hi = x.astype(BF16)
    lo = (x - hi.astype(F32)).astype(BF16)
    return hi, lo


def _prenorm(x, g, shift, scale):
    ms = jnp.mean(x * x, axis=-1, keepdims=True)
    return (x * lax.rsqrt(ms + EPS) * g) * (1.0 + scale) + shift


def _mod_kernel(c_ref, w_ref, b_ref, o_ref):
    a = _silu(c_ref[...]).astype(BF16)
    o_ref[...] = _dot(a, w_ref[...].astype(BF16)) + b_ref[...]


def _modulation(cc, mod_w, mod_b):
    depth, d, n = mod_w.shape
    tn = 6 * LANES * 2
    return pl.pallas_call(
        _mod_kernel,
        grid=(depth, n // tn),
        in_specs=[pl.BlockSpec((MOD_ROWS, d), lambda l, j: (0, 0)),
                  pl.BlockSpec((None, d, tn), lambda l, j: (l, 0, j)),
                  pl.BlockSpec((None, 1, tn), lambda l, j: (l, 0, j))],
        out_specs=pl.BlockSpec((None, MOD_ROWS, tn), lambda l, j: (l, 0, j)),
        out_shape=jax.ShapeDtypeStruct((depth, MOD_ROWS, n), F32),
        compiler_params=_cparams(("parallel", "parallel")),
        name="modulation",
    )(cc, mod_w, mod_b.reshape(depth, 1, n))


def _ev_in_kernel(*refs, n_src, nctx_tiles):
    mod_ref, g_ref, w_ref, qg_ref, kg_ref, cos_ref, sin_ref, q_ref, k_ref, vt_ref, u_ref = refs[n_src:]
    x = _load_stream(refs[:n_src], nctx_tiles)
    hb = _prenorm(x, g_ref[...], mod_ref[0:1, :], mod_ref[1:2, :]).astype(BF16)
    cos = cos_ref[...]
    sin = sin_ref[...]
    lane = lax.broadcasted_iota(jnp.int32, (1, LANES), 1)
    first = (lane % (A_HEAD_DIM // 2)) < (A_HEAD_DIM // 4)

    def norm_rope(t, g):
        y = t * lax.rsqrt(jnp.mean(t * t, axis=-1, keepdims=True) + EPS) * g
        rot = jnp.where(first, pltpu.roll(y, LANES - A_HEAD_DIM // 4, 1), pltpu.roll(y, A_HEAD_DIM // 4, 1))
        return y * cos + rot * sin

    qkv = _dot(hb, w_ref[:, 0:A_Q + 2 * A_KV])
    qg = qg_ref[...] * (A_HEAD_DIM ** -0.5 * LOG2E)
    for h in range(A_HEADS):
        sl = slice(h * LANES, (h + 1) * LANES)
        q_ref[:, sl] = norm_rope(qkv[:, sl], qg).astype(BF16)
    for h in range(A_KV_HEADS):
        sl = slice(h * LANES, (h + 1) * LANES)
        k_ref[:, sl] = norm_rope(qkv[:, A_Q + h * LANES:A_Q + (h + 1) * LANES], kg_ref[...]).astype(BF16)
    for h in range(A_KV_HEADS):
        vt_ref[h] = qkv[:, A_Q + A_KV + h * LANES:A_Q + A_KV + (h + 1) * LANES].T.astype(BF16)
    glu = _dot(hb, w_ref[:, A_Q + 2 * A_KV:])
    u_ref[...] = glu[:, :B_WIDTH] * jax.nn.sigmoid(glu[:, B_WIDTH:])


def _row_specs(b_sz, nctx_tiles, d, layer, first_tile=0):
    mod_spec = pl.BlockSpec((None, None, 6, d),
                            lambda b, i: (layer, jnp.where(i + first_tile < nctx_tiles, b_sz, b), 0, 0))
    g_spec = pl.BlockSpec((None, 1, d), lambda b, i: (layer, 0, 0))
    return mod_spec, g_spec


def _stream_specs(src, nctx_tiles, first_tile=0):
    if not isinstance(src, tuple):
        d = src.shape[-1]
        return [pl.BlockSpec((None, TM, d), lambda b, i: (b, jnp.maximum(i + first_tile, 0), 0))], [src], src.shape
    ctx, x = src
    d = x.shape[-1]
    c_spec = pl.BlockSpec((None, TM, d), lambda b, i: (b, jnp.clip(i + first_tile, 0, nctx_tiles - 1), 0))
    x_spec = pl.BlockSpec((None, TM, d), lambda b, i: (b, jnp.maximum(i + first_tile - nctx_tiles, 0), 0))
    return [c_spec, x_spec], [ctx, x], (x.shape[0], ctx.shape[1] + x.shape[1], d)


def _load_stream(src_refs, nctx_tiles, first_tile=0):
    if len(src_refs) == 1:
        return src_refs[0][...]
    return jnp.where(pl.program_id(1) + first_tile < nctx_tiles, src_refs[0][...], src_refs[1][...])


def _full(shape):
    nd = len(shape)
    return pl.BlockSpec(shape, lambda *_: (0,) * nd, pipeline_mode=pl.Buffered(1))


def _ev_in(src, mod, norm_g, w, qg, kg, cos, sin, layer, nctx_tiles):
    src_specs, src_arrays, (b_sz, t, d) = _stream_specs(src, nctx_tiles)
    mod_spec, g_spec = _row_specs(b_sz, nctx_tiles, d, layer)
    tab = pl.BlockSpec((TM, LANES), lambda b, i: (i, 0))
    out = lambda n: pl.BlockSpec((None, TM, n), lambda b, i: (b, i, 0))
    return pl.pallas_call(
        functools.partial(_ev_in_kernel, n_src=len(src_arrays), nctx_tiles=nctx_tiles),
        grid=(b_sz, t // TM),
        in_specs=src_specs + [mod_spec, g_spec, _full(w.shape), _full(qg.shape), _full(kg.shape), tab, tab],
        out_specs=[out(A_Q), out(A_KV),
                   pl.BlockSpec((None, A_KV_HEADS, None, A_HEAD_DIM, TM), lambda b, i: (b, 0, i, 0, 0)),
                   out(B_WIDTH)],
        out_shape=[jax.ShapeDtypeStruct((b_sz, t, A_Q), BF16),
                   jax.ShapeDtypeStruct((b_sz, t, A_KV), BF16),
                   jax.ShapeDtypeStruct((b_sz, A_KV_HEADS, t // TM, A_HEAD_DIM, TM), BF16),
                   jax.ShapeDtypeStruct((b_sz, t, B_WIDTH), F32)],
        compiler_params=_cparams(("parallel", "parallel"), resident=True),
        name="even_in_proj",
    )(*src_arrays, mod, norm_g, w, qg, kg, cos, sin)


def _ev_attn_kernel(q_ref, k_ref, vt_ref, o_ref, acc_ref, s_ref, *, kb_per_step, n_ctx, nctx_tiles):
    i = pl.program_id(1)
    tm = q_ref.shape[0]
    kb = vt_ref.shape[-1]
    grp = A_HEADS // A_KV_HEADS
    m_rows = grp * tm
    kps = kb_per_step
    nb_ctx = n_ctx // kb
    n_pairs = (vt_ref.shape[1] - nb_ctx) // (2 * kps)

    def group_ops(g):
        gl = slice(g * LANES, (g + 1) * LANES)
        q = jnp.concatenate([q_ref[:, (g * grp + j) * LANES:(g * grp + j + 1) * LANES] for j in range(grp)], axis=0)

        def softmax_pv(s, vt, m, l, first):
            m_new = jnp.maximum(m, jnp.max(s, axis=0, keepdims=True))
            p = jnp.exp2(s - m_new)
            alpha = jnp.exp2(m - m_new)
            l_new = alpha * l + jnp.sum(p, axis=0, keepdims=True)
            pv = _dot(vt, p.astype(BF16))
            acc_ref[g] = pv if first else acc_ref[g] * alpha + pv
            return m_new, l_new

        def finish(l):
            o = (acc_ref[g] / l).T
            for j in range(grp):
                o_ref[:, (g * grp + j) * LANES:(g * grp + j + 1) * LANES] = o[j * tm:(j + 1) * tm].astype(BF16)

        def ctx_scores():
            return _dot_nt(k_ref[0:n_ctx, gl], q)

        def ctx_block(s_ctx):
            vt = jnp.concatenate([vt_ref[g, j] for j in range(nb_ctx)], axis=1)
            return softmax_pv(s_ctx, vt, jnp.full((1, m_rows), NEG, F32), jnp.zeros((1, m_rows), F32), True)

        def scores(blk, slot):
            off = pl.multiple_of(blk * kb, kb)
            s_ref[slot] = _dot_nt(k_ref[pl.ds(off, kps * kb), gl], q)

        def consume(blk, slot, carry):
            vt = jnp.concatenate([vt_ref[g, blk + r] for r in range(kps)], axis=1)
            return softmax_pv(s_ref[slot], vt, *carry, False)

        return finish, ctx_scores, ctx_block, scores, consume

    @pl.when(i < nctx_tiles)
    def _():
        for g in range(A_KV_HEADS):
            finish, ctx_scores, ctx_block, _, _ = group_ops(g)
            _, l_ctx = ctx_block(ctx_scores())
            finish(l_ctx)

    @pl.when(i >= nctx_tiles)
    def _():
        for g in range(A_KV_HEADS):
            finish, ctx_scores, ctx_block, scores, consume = group_ops(g)
            s_ctx = ctx_scores()
            scores(nb_ctx, 0)
            m, l = ctx_block(s_ctx)

            def body(jj, carry):
                blk = nb_ctx + 2 * kps * jj
                scores(blk + kps, 1)
                carry = consume(blk, 0, carry)
                scores(blk + 2 * kps, 0)
                return consume(blk + kps, 1, carry)

            carry = lax.fori_loop(0, n_pairs - 1, body, (m, l))
            blk = nb_ctx + 2 * kps * (n_pairs - 1)
            scores(blk + kps, 1)
            carry = consume(blk, 0, carry)
            _, l_fin = consume(blk + kps, 1, carry)
            finish(l_fin)


def _ev_attn(q, k, vt, n_ctx):
    b_sz, t, _ = q.shape
    nblk, kb = vt.shape[2], vt.shape[4]
    kb_per_step = 4
    assert (nblk - n_ctx // kb) % (2 * kb_per_step) == 0
    grp = A_HEADS // A_KV_HEADS
    kern = functools.partial(_ev_attn_kernel, kb_per_step=kb_per_step, n_ctx=n_ctx, nctx_tiles=n_ctx // TM)
    return pl.pallas_call(
        kern,
        grid=(b_sz, t // TM),
        in_specs=[pl.BlockSpec((None, TM, A_Q), lambda b, i: (b, i, 0)),
                  pl.BlockSpec((None, t, A_KV), lambda b, i: (b, 0, 0)),
                  pl.BlockSpec((None, A_KV_HEADS, nblk, A_HEAD_DIM, kb), lambda b, i: (b, 0, 0, 0, 0))],
        out_specs=pl.BlockSpec((None, TM, A_Q), lambda b, i: (b, i, 0)),
        out_shape=jax.ShapeDtypeStruct((b_sz, t, A_Q), BF16),
        scratch_shapes=[pltpu.VMEM((A_KV_HEADS, A_HEAD_DIM, grp * TM), F32),
                        pltpu.VMEM((2, kb_per_step * kb, grp * TM), F32)],
        compiler_params=_cparams(("parallel", "arbitrary"), resident=True),
        name="even_attention",
    )(q, k, vt)


def _halo_flags(i, nctx_tiles, n_tiles):
    has_prev = jnp.logical_and(i != 0, i != nctx_tiles)
    has_next = jnp.logical_and(i != nctx_tiles - 1, i != n_tiles - 1)
    return has_prev, has_next


def _conv_module_tile(u_ref, up_ref, un_ref, has_prev, has_next, w_ref, b_ref, g_ref, bb_ref, buf, sh, rb, emit):
    tm = u_ref.shape[0]
    buf[0:EV_HALO, :] = jnp.where(has_prev, up_ref[...], 0.0)
    buf[EV_HALO:EV_HALO + tm, :] = u_ref[...]
    buf[EV_HALO + tm:, :] = jnp.where(has_next, un_ref[...], 0.0)
    for s in range(1, SUBLANES):
        sh[s - 1] = buf[pl.ds(s, sh.shape[1]), :]
    base = EV_HALO - B_CONV // 2
    for r in range(tm // rb):
        acc = jnp.zeros((rb, B_WIDTH), F32) + b_ref[...]
        for k in range(B_CONV):
            s = (base + k) % SUBLANES
            al = r * rb + base + k - s
            src = buf[al:al + rb, :] if s == 0 else sh[s - 1, al:al + rb, :]
            acc = acc + src * w_ref[k:k + 1, :]
        mu = jnp.mean(acc, axis=-1, keepdims=True)
        cen = acc - mu
        var = jnp.mean(cen * cen, axis=-1, keepdims=True)
        y = cen * lax.rsqrt(var + EPS) * g_ref[...] + bb_ref[...]
        emit(slice(r * rb, (r + 1) * rb), _silu(y).astype(BF16))


def _ev_conv_kernel(u_ref, up_ref, un_ref, w_ref, b_ref, g_ref, bb_ref, o_ref, buf, sh, *, nctx_tiles, n_tiles, rb):
    has_prev, has_next = _halo_flags(pl.program_id(1), nctx_tiles, n_tiles)

    def emit(rows, v):
        o_ref[rows, :] = v

    _conv_module_tile(u_ref, up_ref, un_ref, has_prev, has_next, w_ref, b_ref, g_ref, bb_ref, buf, sh, rb, emit)


def _halo_specs(t, width, halo):
    per = TM // halo
    last = t // halo - 1
    main = pl.BlockSpec((None, TM, width), lambda b, i: (b, i, 0))
    prev = pl.BlockSpec((None, halo, width), lambda b, i: (b, jnp.maximum(i * per - 1, 0), 0))
    nxt = pl.BlockSpec((None, halo, width), lambda b, i: (b, jnp.minimum((i + 1) * per, last), 0))
    return [main, prev, nxt]


def _ev_conv(u, w, bias, g, bb, n_ctx):
    b_sz, t, c = u.shape
    n_tiles = t // TM
    kern = functools.partial(_ev_conv_kernel, nctx_tiles=n_ctx // TM, n_tiles=n_tiles, rb=EV_CONV_RB)
    return pl.pallas_call(
        kern,
        grid=(b_sz, n_tiles),
        in_specs=_halo_specs(t, c, EV_HALO) + [_full(w.shape), _full(bias.shape), _full(g.shape), _full(bb.shape)],
        out_specs=pl.BlockSpec((None, TM, c), lambda b, i: (b, i, 0)),
        out_shape=jax.ShapeDtypeStruct((b_sz, t, c), BF16),
        scratch_shapes=[pltpu.VMEM((TM + 2 * EV_HALO, c), F32),
                        pltpu.VMEM((SUBLANES - 1, TM + 2 * EV_HALO - SUBLANES, c), F32)],
        compiler_params=_cparams(("parallel", "parallel")),
        name="even_conv_module",
    )(u, u, u, w, bias, g, bb)


def _ffn_tile(x, a1, a2, mod_ref, g_ref, wo1_ref, wo2_ref, wi_ref, wo_ref, act_ref, hid, chunk):
    o = _dot(a1, wo1_ref[...]) + _dot(a2, wo2_ref[...])
    x1 = x + mod_ref[2:3, :] * o
    h2 = _prenorm(x1, g_ref[...], mod_ref[3:4, :], mod_ref[4:5, :]).astype(BF16)
    for c in range(hid // chunk):
        gte = _dot(h2, wi_ref[:, c * chunk:(c + 1) * chunk])
        up = _dot(h2, wi_ref[:, hid + c * chunk:hid + (c + 1) * chunk])
        act_ref[:, c * chunk:(c + 1) * chunk] = (_silu(gte) * up).astype(BF16)
    return x1 + mod_ref[5:6, :] * _dot(act_ref[...], wo_ref[...])


def _out_ffn_kernel(*refs, n_src, nctx_tiles, first_tile, hid, chunk):
    a1_ref, a2_ref, mod_ref, g_ref, wo1_ref, wo2_ref, wi_ref, wo_ref, o_ref, act_ref = refs[n_src:]
    x = _load_stream(refs[:n_src], nctx_tiles, first_tile)
    o_ref[...] = _ffn_tile(x, a1_ref[...], a2_ref[...], mod_ref, g_ref, wo1_ref, wo2_ref, wi_ref, wo_ref,
                           act_ref, hid, chunk)


def _out_ffn(src, a1, a2, mod, norm_g, wo1, wo2, wi, wo, layer, nctx_tiles, latent_only):
    first = nctx_tiles if latent_only else 0
    src_specs, src_arrays, (b_sz, t, d) = _stream_specs(src, nctx_tiles, first)
    hid = wo.shape[0]
    n_tiles = t // TM - first
    mod_spec, g_spec = _row_specs(b_sz, nctx_tiles, d, layer, first)
    row = lambda n: pl.BlockSpec((None, TM, n), lambda b, i: (b, i + first, 0))
    kern = functools.partial(_out_ffn_kernel, n_src=len(src_arrays), nctx_tiles=nctx_tiles, first_tile=first,
                             hid=hid, chunk=2 * LANES)
    in_place = len(src_arrays) == 1 and not latent_only
    return pl.pallas_call(
        kern,
        grid=(b_sz, n_tiles),
        in_specs=src_specs + [row(a1.shape[-1]), row(a2.shape[-1]), mod_spec, g_spec,
                              _full(wo1.shape), _full(wo2.shape), _full(wi.shape), _full(wo.shape)],
        out_specs=pl.BlockSpec((None, TM, d), lambda b, i: (b, i, 0)),
        out_shape=jax.ShapeDtypeStruct((b_sz, n_tiles * TM, d), F32),
        scratch_shapes=[pltpu.VMEM((TM, hid), BF16)],
        input_output_aliases={0: 0} if in_place else {},
        compiler_params=_cparams(("parallel", "parallel"), resident=True),
        name="out_proj_ffn",
    )(*src_arrays, a1, a2, mod, norm_g, wo1, wo2, wi, wo)


def _od_in_kernel(x_ref, mod_ref, g_ref, w_ref, qg_ref, kg_ref, cos_ref, sin_ref, bd_ref, dtb_ref,
                  q_ref, k_ref, vt_ref, xbc_ref, z_ref, dt_ref):
    hb = _prenorm(x_ref[...], g_ref[...], mod_ref[0:1, :], mod_ref[1:2, :]).astype(BF16)
    cos = cos_ref[...]
    sin = sin_ref[...]
    bd = bd_ref[...]
    lane = lax.broadcasted_iota(jnp.int32, (1, LANES), 1)
    first = (lane % (C_HEAD_DIM // 2)) < (C_HEAD_DIM // 4)

    def norm_rope(t, g):
        hi, lo = _split2(t * t)
        ss = _dot(jnp.concatenate([hi, lo], axis=1), bd)
        y = t * lax.rsqrt(ss * (1.0 / C_HEAD_DIM) + EPS) * g
        rot = jnp.where(first, pltpu.roll(y, LANES - C_HEAD_DIM // 4, 1), pltpu.roll(y, C_HEAD_DIM // 4, 1))
        return y * cos + rot * sin

    nq = C_Q // LANES
    nk = 2 * C_KV // LANES
    qkv = _dot(hb, w_ref[:, 0:C_Q + 4 * C_KV])
    qg = qg_ref[...] * (C_HEAD_DIM ** -0.5 * LOG2E)
    for j in range(nq):
        sl = slice(j * LANES, (j + 1) * LANES)
        q_ref[:, sl] = norm_rope(qkv[:, sl], qg).astype(BF16)
    for j in range(nk):
        k_ref[:, j * LANES:(j + 1) * LANES] = norm_rope(qkv[:, C_Q + j * LANES:C_Q + (j + 1) * LANES], kg_ref[...]).astype(BF16)
    for j in range(nk):
        vt_ref[j] = qkv[:, C_Q + 2 * C_KV + j * LANES:C_Q + 2 * C_KV + (j + 1) * LANES].T.astype(BF16)
    rest = _dot(hb, w_ref[:, C_Q + 4 * C_KV:])
    xbc_ref[...] = rest[:, 0:D_XBC]
    z_ref[...] = rest[:, D_XBC:D_XBC + D_INNER]
    dtr = rest[:, D_XBC + D_INNER:] + dtb_ref[...]
    dt_ref[...] = jnp.maximum(dtr, 0.0) + jnp.log1p(jnp.exp(-jnp.abs(dtr)))


def _od_in(xs, mod, norm_g, w, qg, kg, cos, sin, bd, dtb, layer, nctx_tiles):
    b_sz, t, d = xs.shape
    x_spec = pl.BlockSpec((None, TM, d), lambda b, i: (b, i, 0))
    mod_spec, g_spec = _row_specs(b_sz, nctx_tiles, d, layer)
    tab = pl.BlockSpec((TM, LANES), lambda b, i: (i, 0))
    out = lambda n: pl.BlockSpec((None, TM, n), lambda b, i: (b, i, 0))
    return pl.pallas_call(
        _od_in_kernel,
        grid=(b_sz, t // TM),
        in_specs=[x_spec, mod_spec, g_spec, _full(w.shape), _full(qg.shape), _full(kg.shape), tab, tab,
                  _full(bd.shape), _full(dtb.shape)],
        out_specs=[out(C_Q), out(2 * C_KV),
                   pl.BlockSpec((None, C_KV_HEADS, None, LANES, TM), lambda b, i: (b, 0, i, 0, 0)),
                   out(D_XBC), out(D_INNER), out(LANES)],
        out_shape=[jax.ShapeDtypeStruct((b_sz, t, C_Q), BF16),
                   jax.ShapeDtypeStruct((b_sz, t, 2 * C_KV), BF16),
                   jax.ShapeDtypeStruct((b_sz, C_KV_HEADS, t // TM, LANES, TM), BF16),
                   jax.ShapeDtypeStruct((b_sz, t, D_XBC), F32),
                   jax.ShapeDtypeStruct((b_sz, t, D_INNER), F32),
                   jax.ShapeDtypeStruct((b_sz, t, LANES), F32)],
        compiler_params=_cparams(("parallel", "parallel"), resident=True),
        name="odd_in_proj",
    )(xs, mod, norm_g, w, qg, kg, cos, sin, bd, dtb)


def _od_attn_kernel(sink_ref, q_ref, k_ref, vt_ref, o_ref, s_ref, *, n_ctx, t, nctx_tiles):
    i = pl.program_id(1)
    tm = q_ref.shape[0]
    w = C_WINDOW
    n_tiles = vt_ref.shape[1]
    grp = C_HEADS // C_KV_HEADS
    span = tm + 2 * w
    nkeys = n_ctx + span
    prev = jnp.maximum(i - 1, 0)
    nxt = jnp.minimum(i + 1, n_tiles - 1)
    zero = jnp.zeros((), BF16)
    low_lane = lax.broadcasted_iota(jnp.int32, (1, LANES), 1) < C_HEAD_DIM
    low_row = lax.broadcasted_iota(jnp.int32, (LANES, 1), 0) < C_HEAD_DIM
    k2, vt2 = [], []
    for g in range(C_KV_HEADS):
        gl = slice(g * LANES, (g + 1) * LANES)
        k_all = jnp.concatenate([k_ref[0:n_ctx, gl],
                                 k_ref[pl.ds(pl.multiple_of(prev * tm + w, w), w), gl],
                                 k_ref[pl.ds(pl.multiple_of(i * tm, tm), tm), gl],
                                 k_ref[pl.ds(pl.multiple_of(nxt * tm, tm), w), gl]], axis=0)
        vt_all = jnp.concatenate([vt_ref[g, j] for j in range(n_ctx // tm)]
                                 + [vt_ref[g, prev][:, w:], vt_ref[g, i], vt_ref[g, nxt][:, :w]], axis=1)
        k2.append(jnp.concatenate([jnp.where(low_lane, k_all, zero), jnp.where(low_lane, zero, k_all)], axis=0))
        vt2.append(jnp.concatenate([jnp.where(low_row, vt_all, zero), jnp.where(low_row, zero, vt_all)], axis=1))
    rr = lax.broadcasted_iota(jnp.int32, (span, tm), 0)
    cc = lax.broadcasted_iota(jnp.int32, (span, tm), 1)
    kpos = i * tm - w + rr
    ok = jnp.logical_and(jnp.abs(rr - w - cc) <= w, jnp.logical_and(kpos >= n_ctx, kpos < t))
    bias = jnp.where(jnp.logical_and(ok, i >= nctx_tiles), 0.0, NEG)
    n_pairs = C_HEADS // 2

    def scores(n):
        s_ref[n % 2] = _dot_nt(k2[n // (grp // 2)], q_ref[:, n * LANES:(n + 1) * LANES])

    scores(0)
    for n in range(n_pairs):
        if n + 1 < n_pairs:
            scores(n + 1)
        ps, invs = [], []
        for par in range(2):
            sk = sink_ref[2 * n + par] * LOG2E
            s_c = s_ref[n % 2, par * nkeys:par * nkeys + n_ctx, :]
            s_l = s_ref[n % 2, par * nkeys + n_ctx:(par + 1) * nkeys, :] + bias
            m = jnp.maximum(jnp.maximum(jnp.max(s_c, axis=0, keepdims=True), jnp.max(s_l, axis=0, keepdims=True)), sk)
            p_c = jnp.exp2(s_c - m)
            p_l = jnp.exp2(s_l - m)
            den = jnp.sum(p_c, axis=0, keepdims=True) + jnp.sum(p_l, axis=0, keepdims=True) + jnp.exp2(sk - m)
            ps += [p_c.astype(BF16), p_l.astype(BF16)]
            invs.append(1.0 / den)
        o_t = _dot(vt2[n // (grp // 2)], jnp.concatenate(ps, axis=0)) * jnp.where(low_row, invs[0], invs[1])
        o_ref[:, n * LANES:(n + 1) * LANES] = o_t.T.astype(BF16)


def _od_attn(sink, q, k, vt, n_ctx):
    b_sz, t, _ = q.shape
    assert TM == 2 * C_WINDOW
    kern = functools.partial(_od_attn_kernel, n_ctx=n_ctx, t=t, nctx_tiles=n_ctx // TM)
    return pl.pallas_call(
        kern,
        grid=(b_sz, t // TM),
        in_specs=[pl.BlockSpec(memory_space=pltpu.SMEM),
                  pl.BlockSpec((None, TM, C_Q), lambda b, i: (b, i, 0)),
                  pl.BlockSpec((None, t, 2 * C_KV), lambda b, i: (b, 0, 0)),
                  pl.BlockSpec((None, C_KV_HEADS, t // TM, LANES, TM), lambda b, i: (b, 0, 0, 0, 0))],
        out_specs=pl.BlockSpec((None, TM, C_Q), lambda b, i: (b, i, 0)),
        out_shape=jax.ShapeDtypeStruct((b_sz, t, C_Q), BF16),
        scratch_shapes=[pltpu.VMEM((2, 2 * (n_ctx + TM + 2 * C_WINDOW), TM), F32)],
        compiler_params=_cparams(("parallel", "arbitrary"), resident=True),
        name="odd_attention",
    )(sink, q, k, vt)


def _od_conv_kernel(u_ref, up_ref, un_ref, w_ref, b_ref, xs_ref, bc_ref, buf, *, nctx_tiles, n_tiles, rb):
    i = pl.program_id(1)
    has_prev, has_next = _halo_flags(i, nctx_tiles, n_tiles)
    tm = u_ref.shape[0]
    buf[0:OD_HALO, :] = jnp.where(has_prev, up_ref[...], 0.0)
    buf[OD_HALO:OD_HALO + tm, :] = u_ref[...]
    buf[OD_HALO + tm:, :] = jnp.where(has_next, un_ref[...], 0.0)
    base = OD_HALO - D_CONV // 2
    cw = bc_ref.shape[1]
    for r in range(tm // rb):
        rows = slice(r * rb, (r + 1) * rb)
        for cb in range(D_XBC // cw):
            cs = slice(cb * cw, (cb + 1) * cw)
            acc = jnp.zeros((rb, cw), F32) + b_ref[:, cs]
            for k in range(D_CONV):
                acc = acc + buf[pl.ds(r * rb + base + k, rb), cs] * w_ref[k:k + 1, cs]
            y = _silu(acc)
            if cb * cw < D_INNER:
                xs_ref[rows, cs] = y
            else:
                bc_ref[rows, :] = y.astype(BF16)


def _od_conv(xbc, w, bias, n_ctx):
    b_sz, t, c = xbc.shape
    n_tiles = t // TM
    kern = functools.partial(_od_conv_kernel, nctx_tiles=n_ctx // TM, n_tiles=n_tiles, rb=TM)
    out = lambda n: pl.BlockSpec((None, TM, n), lambda b, i: (b, i, 0))
    return pl.pallas_call(
        kern,
        grid=(b_sz, n_tiles),
        in_specs=_halo_specs(t, c, OD_HALO) + [_full(w.shape), _full(bias.shape)],
        out_specs=[out(D_INNER), out(c - D_INNER)],
        out_shape=[jax.ShapeDtypeStruct((b_sz, t, D_INNER), F32),
                   jax.ShapeDtypeStruct((b_sz, t, c - D_INNER), BF16)],
        scratch_shapes=[pltpu.VMEM((TM + 2 * OD_HALO, c), F32)],
        compiler_params=_cparams(("parallel", "parallel")),
        name="odd_ssd_conv",
    )(xbc, xbc, xbc, w, bias)


def _ssd_kernel(*refs, reverse, gated):
    if gated:
        xs_ref, bc_ref, dt_ref, av_ref, ex_ref, dsk_ref, y1_ref, z_ref, gn_ref, o_ref, h_ref = refs
    else:
        xs_ref, bc_ref, dt_ref, av_ref, ex_ref, dsk_ref, o_ref, h_ref = refs
    q = D_CHUNK
    gw = D_INNER // D_GROUPS
    hpg = D_HEADS // D_GROUPS
    col0 = D_HEADS if reverse else 0

    @pl.when(pl.program_id(1) == 0)
    def _():
        h_ref[...] = jnp.zeros_like(h_ref)

    ii = lax.broadcasted_iota(jnp.int32, (q, q), 0)
    jj = lax.broadcasted_iota(jnp.int32, (q, q), 1)
    tri = (jj >= ii) if reverse else (jj <= ii)
    tri_b = jnp.where(tri, 1.0, 0.0).astype(BF16)
    low = lax.broadcasted_iota(jnp.int32, (1, LANES), 1) < D_HEAD_DIM
    edge = 0 if reverse else q - 1

    def expand(w):
        hi, lo = _split2(w)
        return _dot(jnp.concatenate([hi, lo], axis=1), ex_ref[...])

    n_chunks = xs_ref.shape[0] // q
    for ch in (range(n_chunks - 1, -1, -1) if reverse else range(n_chunks)):
        rows = slice(ch * q, (ch + 1) * q)
        dt = dt_ref[rows, :]
        a = dt * av_ref[...]
        a1 = a.astype(BF16)
        r1 = a - a1.astype(F32)
        a2 = r1.astype(BF16)
        a3 = (r1 - a2.astype(F32)).astype(BF16)
        cs = _dot(tri_b, jnp.concatenate([a1, a2, a3], axis=1))
        acs = cs[:, 0:LANES] + cs[:, LANES:2 * LANES] + cs[:, 2 * LANES:3 * LANES]
        tot = acs[edge:edge + 1, :]
        xs = xs_ref[rows, :]
        w_y = expand(jnp.exp(acs))
        xdt = (xs * expand(dt)).astype(BF16)
        xw = (xs * expand(dt * jnp.exp(tot - acs))).astype(BF16)
        acs_t = acs.T
        bc = bc_ref[rows, :]
        h_in = h_ref[...]
        hb = h_in.astype(BF16)
        ys = []
        for g in range(D_GROUPS):
            gs = slice(g * gw, (g + 1) * gw)
            b_g = bc[:, g * D_STATE:(g + 1) * D_STATE]
            c_g = bc[:, (D_GROUPS + g) * D_STATE:(D_GROUPS + g + 1) * D_STATE]
            cb = _dot_nt(c_g, b_g)
            inter = _dot(c_g, hb[:, gs]) * w_y[:, gs]
            for pr in range(hpg // 2):
                x2 = xdt[:, g * gw + pr * LANES:g * gw + (pr + 1) * LANES]
                outs = []
                for par in range(2):
                    c = col0 + g * hpg + 2 * pr + par
                    seg = acs[:, c:c + 1] - acs_t[c:c + 1, :]
                    lmat = (jnp.exp(jnp.where(tri, seg, NEG)) * cb).astype(BF16)
                    outs.append(_dot(lmat, x2))
                ys.append(jnp.where(low, outs[0], outs[1]) + inter[:, pr * LANES:(pr + 1) * LANES])
            h_ref[:, gs] = h_in[:, gs] * w_y[edge:edge + 1, gs] + _dot_tn(b_g, xw[:, gs])
        y = jnp.concatenate(ys, axis=1)
        if gated:
            gz = (y1_ref[rows, :] + y) * _silu(z_ref[rows, :])
            ms = jnp.mean(gz * gz, axis=-1, keepdims=True)
            o_ref[rows, :] = (gz * lax.rsqrt(ms + EPS) * gn_ref[...]).astype(BF16)
        else:
            o_ref[rows, :] = dsk_ref[...] * xs + y


def _ssd(xs, bc, dt, avec, ex, dsk, n_ctx, reverse, y1=None, z=None, gn=None):
    b_sz, t, _ = xs.shape
    q = TM
    nc = t // q
    ncc = n_ctx // q
    if reverse:
        cmap = lambda b, s: (b, jnp.where(s < ncc, ncc - 1 - s, nc - 1 - (s - ncc)), 0)
    else:
        cmap = lambda b, s: (b, s, 0)
    row = lambda n: pl.BlockSpec((None, q, n), cmap)
    gated = y1 is not None
    args = [xs, bc, dt, avec, ex, dsk]
    specs = [row(D_INNER), row(bc.shape[-1]), row(LANES), _full(avec.shape), _full(ex.shape), _full(dsk.shape)]
    if gated:
        args += [y1, z, gn]
        specs += [row(D_INNER), row(D_INNER), _full(gn.shape)]
    return pl.pallas_call(
        functools.partial(_ssd_kernel, reverse=reverse, gated=gated),
        grid=(b_sz, nc),
        in_specs=specs,
        out_specs=row(D_INNER),
        out_shape=jax.ShapeDtypeStruct((b_sz, t, D_INNER), BF16 if gated else F32),
        scratch_shapes=[pltpu.VMEM((D_STATE, D_INNER), F32)],
        compiler_params=_cparams(("parallel", "arbitrary")),
        name="ssd_scan_bwd_gate" if gated else "ssd_scan_fwd",
    )(*args)


def _rope_tables(seq, n_ctx, head_dim):
    pos = jnp.arange(seq)
    row = (pos // GRID_W).astype(F32)
    col = (pos % GRID_W).astype(F32)
    quarter = head_dim // 4
    inv = ROPE_THETA ** (-jnp.arange(quarter, dtype=F32) / quarter)
    ar = row[:, None] * inv
    ac = col[:, None] * inv
    ang = jnp.concatenate([ar, ar, ac, ac], axis=-1)
    sign = jnp.where((jnp.arange(head_dim) % (head_dim // 2)) < quarter, -1.0, 1.0).astype(F32)
    cos = jnp.concatenate([jnp.ones((n_ctx, head_dim), F32), jnp.cos(ang)], axis=0)
    sin = jnp.concatenate([jnp.zeros((n_ctx, head_dim), F32), jnp.sin(ang) * sign], axis=0)
    rep = LANES // head_dim
    return jnp.tile(cos, (1, rep)), jnp.tile(sin, (1, rep))


def _dup_heads(w, head_dim):
    d, n = w.shape
    w = w.reshape(d, n // head_dim, 1, head_dim)
    return jnp.broadcast_to(w, (d, n // head_dim, 2, head_dim)).reshape(d, 2 * n)


def kernel(x, c, ctx, c_ctx, mod_w, mod_b, norm1_g, norm2_g, ffn_w_in, ffn_w_out, ev_w_in, ev_w_out, ev_q_g, ev_k_g, ev_dw_w, ev_dw_b, ev_cn_g, ev_cn_b, od_w_in, od_w_out, od_q_g, od_k_g, od_sink, od_conv_w, od_conv_b, od_dt_bias, od_A_log, od_D, od_gnorm_g):
    b_sz, seq, d = x.shape
    n_ctx = ctx.shape[1]
    depth = mod_w.shape[0]
    assert seq % TM == 0 and n_ctx % TM == 0 and seq >= TM + 2 * C_WINDOW and b_sz + 1 <= MOD_ROWS
    nctx_tiles = n_ctx // TM

    xs = (ctx, x)
    cc = jnp.zeros((MOD_ROWS, d), F32).at[:b_sz].set(c).at[b_sz].set(c_ctx)
    mod = _modulation(cc, mod_w, mod_b).reshape(depth, MOD_ROWS, 6, d)
    g1 = norm1_g.reshape(depth, 1, d)
    g2 = norm2_g.reshape(depth, 1, d)

    cos_a, sin_a = _rope_tables(seq, n_ctx, A_HEAD_DIM)
    cos_c, sin_c = _rope_tables(seq, n_ctx, C_HEAD_DIM)
    lane = jnp.arange(LANES)
    bd = (lane[:, None] // C_HEAD_DIM == lane[None, :] // C_HEAD_DIM).astype(BF16)
    bd = jnp.concatenate([bd, bd], axis=0)
    head_of_lane = jnp.arange(D_INNER) // D_HEAD_DIM
    ex = []
    for dirn in range(2):
        e = (lane[:, None] == head_of_lane[None, :] + dirn * D_HEADS).astype(BF16)
        ex.append(jnp.concatenate([e, e], axis=0))

    for l in range(depth):
        i = l // 2
        wi = ffn_w_in[l].astype(BF16)
        wo = ffn_w_out[l].astype(BF16)
        if l % 2 == 0:
            w_in = ev_w_in[i].astype(BF16)
            w_out = ev_w_out[i].astype(BF16)
            q, k, v, u = _ev_in(xs, mod, g1, w_in, ev_q_g[i].reshape(1, -1), ev_k_g[i].reshape(1, -1),
                                cos_a, sin_a, l, nctx_tiles)
            a1 = _ev_attn(q, k, v, n_ctx)
            conv_p = (ev_dw_w[i], ev_dw_b[i].reshape(1, -1), ev_cn_g[i].reshape(1, -1), ev_cn_b[i].reshape(1, -1))
            wo1, wo2 = w_out[:A_Q], w_out[A_Q:]
            a2 = _ev_conv(u, *conv_p, n_ctx)
        else:
            w = od_w_in[i]
            o_k, o_v, o_x, o_dt, o_z = C_Q, C_Q + C_KV, C_Q + 2 * C_KV, C_Q + 2 * C_KV + D_XBC, C_Q + 2 * C_KV + D_XBC + 2 * D_HEADS
            w_in = jnp.concatenate([
                w[:, :o_k], _dup_heads(w[:, o_k:o_v], C_HEAD_DIM), _dup_heads(w[:, o_v:o_x], C_HEAD_DIM),
                w[:, o_x:o_dt], w[:, o_z:], w[:, o_dt:o_z], jnp.zeros((d, LANES - 2 * D_HEADS), F32)], axis=1).astype(BF16)
            w_out = od_w_out[i].astype(BF16)
            pad = jnp.zeros((LANES - 2 * D_HEADS,), F32)
            dtb = jnp.concatenate([od_dt_bias[i].reshape(-1), pad]).reshape(1, LANES)
            avec = jnp.concatenate([-jnp.exp(od_A_log[i].astype(F32)).reshape(-1), pad]).reshape(1, LANES)
            dsk = jnp.repeat(od_D[i].astype(F32), D_HEAD_DIM).reshape(1, D_INNER)
            qg = jnp.tile(od_q_g[i], LANES // C_HEAD_DIM).reshape(1, LANES)
            kg = jnp.tile(od_k_g[i], LANES // C_HEAD_DIM).reshape(1, LANES)
            q, k, v, xbc, z, dt = _od_in(xs, mod, g1, w_in, qg, kg, cos_c, sin_c, bd, dtb, l, nctx_tiles)
            a1 = _od_attn(od_sink[i], q, k, v, n_ctx)
            xc, bc = _od_conv(xbc, od_conv_w[i], od_conv_b[i].reshape(1, -1), n_ctx)
            y1 = _ssd(xc, bc, dt, avec, ex[0], dsk, n_ctx, reverse=False)
            a2 = _ssd(xc, bc, dt, avec, ex[1], dsk, n_ctx, reverse=True, y1=y1, z=z,
                      gn=od_gnorm_g[i].reshape(1, -1))
            wo1, wo2 = w_out[:C_Q], w_out[C_Q:]
        xs = _out_ffn(xs, a1, a2, mod, g2, wo1, wo2, wi, wo, l, nctx_tiles, latent_only=(l == depth - 1))
    return xs
```

```python
import functools
import math

import jax
import jax.numpy as jnp
from jax import lax
from jax.experimental import pallas as pl
from jax.experimental.pallas import tpu as pltpu

F32 = jnp.float32
BF16 = jnp.bfloat16

EPS = 1e-6
ROPE_THETA = 10000.0
GRID_W = 64
LOG2E = math.log2(math.e)
NEG = -1e30

A_HEADS, A_KV_HEADS, A_HEAD_DIM = 8, 2, 128
B_WIDTH, B_CONV = 512, 31
C_HEADS, C_KV_HEADS, C_HEAD_DIM, C_WINDOW = 16, 2, 64, 128
D_INNER, D_HEAD_DIM, D_HEADS, D_STATE, D_GROUPS, D_CONV, D_CHUNK = 1024, 64, 16, 128, 2, 5, 128
A_Q, A_KV = A_HEADS * A_HEAD_DIM, A_KV_HEADS * A_HEAD_DIM
C_Q, C_KV = C_HEADS * C_HEAD_DIM, C_KV_HEADS * C_HEAD_DIM
D_XBC = D_INNER + 2 * D_GROUPS * D_STATE

LANES = 128
SUBLANES = 8
VMEM_BYTES = 64 * 1024 * 1024
VMEM_LIMIT_RESIDENT = VMEM_BYTES * 7 // 8

TM = 256
MOD_ROWS = 16
EV_HALO = 16
OD_HALO = 8
EV_CONV_RB = TM


def _cparams(sem, resident=False):
    kw = dict(dimension_semantics=sem)
    if resident:
        kw["vmem_limit_bytes"] = VMEM_LIMIT_RESIDENT
    return pltpu.CompilerParams(**kw)


def _dot(a, b):
    return jnp.dot(a, b, preferred_element_type=F32)


def _dot_nt(a, b):
    return lax.dot_general(a, b, (((1,), (1,)), ((), ())), preferred_element_type=F32)


def _dot_tn(a, b):
    return lax.dot_general(a, b, (((0,), (0,)), ((), ())), preferred_element_type=F32)


def _silu(x):
    return x * jax.nn.sigmoid(x)


def _split2(x):
    hi = x.astype(BF16)
    lo = (x - hi.astype(F32)).astype(BF16)
    return hi, lo


def _prenorm(x, g, shift, scale):
    ms = jnp.mean(x * x, axis=-1, keepdims=True)
    return (x * lax.rsqrt(ms + EPS) * g) * (1.0 + scale) + shift


def _mod_kernel(c_ref, w_ref, b_ref, o_ref):
    a = _silu(c_ref[...]).astype(BF16)
    o_ref[...] = _dot(a, w_ref[...].astype(BF16)) + b_ref[...]


def _modulation(cc, mod_w, mod_b):
    depth, d, n = mod_w.shape
    tn = 6 * LANES * 2
    return pl.pallas_call(
        _mod_kernel,
        grid=(depth, n // tn),
        in_specs=[pl.BlockSpec((MOD_ROWS, d), lambda l, j: (0, 0)),
                  pl.BlockSpec((None, d, tn), lambda l, j: (l, 0, j)),
                  pl.BlockSpec((None, 1, tn), lambda l, j: (l, 0, j))],
        out_specs=pl.BlockSpec((None, MOD_ROWS, tn), lambda l, j: (l, 0, j)),
        out_shape=jax.ShapeDtypeStruct((depth, MOD_ROWS, n), F32),
        compiler_params=_cparams(("parallel", "parallel")),
        name="modulation",
    )(cc, mod_w, mod_b.reshape(depth, 1, n))


def _ev_in_kernel(*refs, n_src, nctx_tiles):
    mod_ref, g_ref, w_ref, qg_ref, kg_ref, cos_ref, sin_ref, q_ref, k_ref, vt_ref, u_ref = refs[n_src:]
    x = _load_stream(refs[:n_src], nctx_tiles)
    hb = _prenorm(x, g_ref[...], mod_ref[0:1, :], mod_ref[1:2, :]).astype(BF16)
    cos = cos_ref[...]
    sin = sin_ref[...]
    lane = lax.broadcasted_iota(jnp.int32, (1, LANES), 1)
    first = (lane % (A_HEAD_DIM // 2)) < (A_HEAD_DIM // 4)

    def norm_rope(t, g):
        y = t * lax.rsqrt(jnp.mean(t * t, axis=-1, keepdims=True) + EPS) * g
        rot = jnp.where(first, pltpu.roll(y, LANES - A_HEAD_DIM // 4, 1), pltpu.roll(y, A_HEAD_DIM // 4, 1))
        return y * cos + rot * sin

    qkv = _dot(hb, w_ref[:, 0:A_Q + 2 * A_KV])
    qg = qg_ref[...] * (A_HEAD_DIM ** -0.5 * LOG2E)
    for h in range(A_HEADS):
        sl = slice(h * LANES, (h + 1) * LANES)
        q_ref[:, sl] = norm_rope(qkv[:, sl], qg).astype(BF16)
    for h in range(A_KV_HEADS):
        sl = slice(h * LANES, (h + 1) * LANES)
        k_ref[:, sl] = norm_rope(qkv[:, A_Q + h * LANES:A_Q + (h + 1) * LANES], kg_ref[...]).astype(BF16)
    for h in range(A_KV_HEADS):
        vt_ref[h] = qkv[:, A_Q + A_KV + h * LANES:A_Q + A_KV + (h + 1) * LANES].T.astype(BF16)
    glu = _dot(hb, w_ref[:, A_Q + 2 * A_KV:])
    u_ref[...] = glu[:, :B_WIDTH] * jax.nn.sigmoid(glu[:, B_WIDTH:])


def _row_specs(b_sz, nctx_tiles, d, layer, first_tile=0):
    mod_spec = pl.BlockSpec((None, None, 6, d),
                            lambda b, i: (layer, jnp.where(i + first_tile < nctx_tiles, b_sz, b), 0, 0))
    g_spec = pl.BlockSpec((None, 1, d), lambda b, i: (layer, 0, 0))
    return mod_spec, g_spec


def _stream_specs(src, nctx_tiles, first_tile=0):
    if not isinstance(src, tuple):
        d = src.shape[-1]
        return [pl.BlockSpec((None, TM, d), lambda b, i: (b, jnp.maximum(i + first_tile, 0), 0))], [src], src.shape
    ctx, x = src
    d = x.shape[-1]
    c_spec = pl.BlockSpec((None, TM, d), lambda b, i: (b, jnp.clip(i + first_tile, 0, nctx_tiles - 1), 0))
    x_spec = pl.BlockSpec((None, TM, d), lambda b, i: (b, jnp.maximum(i + first_tile - nctx_tiles, 0), 0))
    return [c_spec, x_spec], [ctx, x], (x.shape[0], ctx.shape[1] + x.shape[1], d)


def _load_stream(src_refs, nctx_tiles, first_tile=0):
    if len(src_refs) == 1:
        return src_refs[0][...]
    return jnp.where(pl.program_id(1) + first_tile < nctx_tiles, src_refs[0][...], src_refs[1][...])


def _full(shape):
    nd = len(shape)
    return pl.BlockSpec(shape, lambda *_: (0,) * nd, pipeline_mode=pl.Buffered(1))


def _ev_in(src, mod, norm_g, w, qg, kg, cos, sin, layer, nctx_tiles):
    src_specs, src_arrays, (b_sz, t, d) = _stream_specs(src, nctx_tiles)
    mod_spec, g_spec = _row_specs(b_sz, nctx_tiles, d, layer)
    tab = pl.BlockSpec((TM, LANES), lambda b, i: (i, 0))
    out = lambda n: pl.BlockSpec((None, TM, n), lambda b, i: (b, i, 0))
    return pl.pallas_call(
        functools.partial(_ev_in_kernel, n_src=len(src_arrays), nctx_tiles=nctx_tiles),
        grid=(b_sz, t // TM),
        in_specs=src_specs + [mod_spec, g_spec, _full(w.shape), _full(qg.shape), _full(kg.shape), tab, tab],
        out_specs=[out(A_Q), out(A_KV),
                   pl.BlockSpec((None, A_KV_HEADS, None, A_HEAD_DIM, TM), lambda b, i: (b, 0, i, 0, 0)),
                   out(B_WIDTH)],
        out_shape=[jax.ShapeDtypeStruct((b_sz, t, A_Q), BF16),
                   jax.ShapeDtypeStruct((b_sz, t, A_KV), BF16),
                   jax.ShapeDtypeStruct((b_sz, A_KV_HEADS, t // TM, A_HEAD_DIM, TM), BF16),
                   jax.ShapeDtypeStruct((b_sz, t, B_WIDTH), F32)],
        compiler_params=_cparams(("parallel", "parallel"), resident=True),
        name="even_in_proj",
    )(*src_arrays, mod, norm_g, w, qg, kg, cos, sin)


def _ev_attn_kernel(q_ref, k_ref, vt_ref, o_ref, acc_ref, s_ref, *, kb_per_step, n_ctx, nctx_tiles):
    i = pl.program_id(1)
    tm = q_ref.shape[0]
    kb = vt_ref.shape[-1]
    grp = A_HEADS // A_KV_HEADS
    m_rows = grp * tm
    kps = kb_per_step
    nb_ctx = n_ctx // kb
    n_pairs = (vt_ref.shape[1] - nb_ctx) // (2 * kps)

    def group_ops(g):
        gl = slice(g * LANES, (g + 1) * LANES)
        q = jnp.concatenate([q_ref[:, (g * grp + j) * LANES:(g * grp + j + 1) * LANES] for j in range(grp)], axis=0)

        def softmax_pv(s, vt, m, l, first):
            m_new = jnp.maximum(m, jnp.max(s, axis=0, keepdims=True))
            p = jnp.exp2(s - m_new)
            alpha = jnp.exp2(m - m_new)
            l_new = alpha * l + jnp.sum(p, axis=0, keepdims=True)
            pv = _dot(vt, p.astype(BF16))
            acc_ref[g] = pv if first else acc_ref[g] * alpha + pv
            return m_new, l_new

        def finish(l):
            o = (acc_ref[g] / l).T
            for j in range(grp):
                o_ref[:, (g * grp + j) * LANES:(g * grp + j + 1) * LANES] = o[j * tm:(j + 1) * tm].astype(BF16)

        def ctx_scores():
            return _dot_nt(k_ref[0:n_ctx, gl], q)

        def ctx_block(s_ctx):
            vt = jnp.concatenate([vt_ref[g, j] for j in range(nb_ctx)], axis=1)
            return softmax_pv(s_ctx, vt, jnp.full((1, m_rows), NEG, F32), jnp.zeros((1, m_rows), F32), True)

        def scores(blk, slot):
            off = pl.multiple_of(blk * kb, kb)
            s_ref[slot] = _dot_nt(k_ref[pl.ds(off, kps * kb), gl], q)

        def consume(blk, slot, carry):
            vt = jnp.concatenate([vt_ref[g, blk + r] for r in range(kps)], axis=1)
            return softmax_pv(s_ref[slot], vt, *carry, False)

        return finish, ctx_scores, ctx_block, scores, consume

    @pl.when(i < nctx_tiles)
    def _():
        for g in range(A_KV_HEADS):
            finish, ctx_scores, ctx_block, _, _ = group_ops(g)
            _, l_ctx = ctx_block(ctx_scores())
            finish(l_ctx)

    @pl.when(i >= nctx_tiles)
    def _():
        for g in range(A_KV_HEADS):
            finish, ctx_scores, ctx_block, scores, consume = group_ops(g)
            s_ctx = ctx_scores()
            scores(nb_ctx, 0)
            m, l = ctx_block(s_ctx)

            def body(jj, carry):
                blk = nb_ctx + 2 * kps * jj
                scores(blk + kps, 1)
                carry = consume(blk, 0, carry)
                scores(blk + 2 * kps, 0)
                return consume(blk + kps, 1, carry)

            carry = lax.fori_loop(0, n_pairs - 1, body, (m, l))
            blk = nb_ctx + 2 * kps * (n_pairs - 1)
            scores(blk + kps, 1)
            carry = consume(blk, 0, carry)
            _, l_fin = consume(blk + kps, 1, carry)
            finish(l_fin)


def _ev_attn(q, k, vt, n_ctx):
    b_sz, t, _ = q.shape
    nblk, kb = vt.shape[2], vt.shape[4]
    kb_per_step = 4
    assert (nblk - n_ctx // kb) % (2 * kb_per_step) == 0
    grp = A_HEADS // A_KV_HEADS
    kern = functools.partial(_ev_attn_kernel, kb_per_step=kb_per_step, n_ctx=n_ctx, nctx_tiles=n_ctx // TM)
    return pl.pallas_call(
        kern,
        grid=(b_sz, t // TM),
        in_specs=[pl.BlockSpec((None, TM, A_Q), lambda b, i: (b, i, 0)),
                  pl.BlockSpec((None, t, A_KV), lambda b, i: (b, 0, 0)),
                  pl.BlockSpec((None, A_KV_HEADS, nblk, A_HEAD_DIM, kb), lambda b, i: (b, 0, 0, 0, 0))],
        out_specs=pl.BlockSpec((None, TM, A_Q), lambda b, i: (b, i, 0)),
        out_shape=jax.ShapeDtypeStruct((b_sz, t, A_Q), BF16),
        scratch_shapes=[pltpu.VMEM((A_KV_HEADS, A_HEAD_DIM, grp * TM), F32),
                        pltpu.VMEM((2, kb_per_step * kb, grp * TM), F32)],
        compiler_params=_cparams(("parallel", "arbitrary"), resident=True),
        name="even_attention",
    )(q, k, vt)


def _halo_flags(i, nctx_tiles, n_tiles):
    has_prev = jnp.logical_and(i != 0, i != nctx_tiles)
    has_next = jnp.logical_and(i != nctx_tiles - 1, i != n_tiles - 1)
    return has_prev, has_next


def _conv_module_tile(u_ref, up_ref, un_ref, has_prev, has_next, w_ref, b_ref, g_ref, bb_ref, buf, sh, rb, emit):
    tm = u_ref.shape[0]
    buf[0:EV_HALO, :] = jnp.where(has_prev, up_ref[...], 0.0)
    buf[EV_HALO:EV_HALO + tm, :] = u_ref[...]
    buf[EV_HALO + tm:, :] = jnp.where(has_next, un_ref[...], 0.0)
    for s in range(1, SUBLANES):
        sh[s - 1] = buf[pl.ds(s, sh.shape[1]), :]
    base = EV_HALO - B_CONV // 2
    for r in range(tm // rb):
        acc = jnp.zeros((rb, B_WIDTH), F32) + b_ref[...]
        for k in range(B_CONV):
            s = (base + k) % SUBLANES
            al = r * rb + base + k - s
            src = buf[al:al + rb, :] if s == 0 else sh[s - 1, al:al + rb, :]
            acc = acc + src * w_ref[k:k + 1, :]
        mu = jnp.mean(acc, axis=-1, keepdims=True)
        cen = acc - mu
        var = jnp.mean(cen * cen, axis=-1, keepdims=True)
        y = cen * lax.rsqrt(var + EPS) * g_ref[...] + bb_ref[...]
        emit(slice(r * rb, (r + 1) * rb), _silu(y).astype(BF16))


def _ev_conv_kernel(u_ref, up_ref, un_ref, w_ref, b_ref, g_ref, bb_ref, o_ref, buf, sh, *, nctx_tiles, n_tiles, rb):
    has_prev, has_next = _halo_flags(pl.program_id(1), nctx_tiles, n_tiles)

    def emit(rows, v):
        o_ref[rows, :] = v

    _conv_module_tile(u_ref, up_ref, un_ref, has_prev, has_next, w_ref, b_ref, g_ref, bb_ref, buf, sh, rb, emit)


def _halo_specs(t, width, halo):
    per = TM // halo
    last = t // halo - 1
    main = pl.BlockSpec((None, TM, width), lambda b, i: (b, i, 0))
    prev = pl.BlockSpec((None, halo, width), lambda b, i: (b, jnp.maximum(i * per - 1, 0), 0))
    nxt = pl.BlockSpec((None, halo, width), lambda b, i: (b, jnp.minimum((i + 1) * per, last), 0))
    return [main, prev, nxt]


def _ev_conv(u, w, bias, g, bb, n_ctx):
    b_sz, t, c = u.shape
    n_tiles = t // TM
    kern = functools.partial(_ev_conv_kernel, nctx_tiles=n_ctx // TM, n_tiles=n_tiles, rb=EV_CONV_RB)
    return pl.pallas_call(
        kern,
        grid=(b_sz, n_tiles),
        in_specs=_halo_specs(t, c, EV_HALO) + [_full(w.shape), _full(bias.shape), _full(g.shape), _full(bb.shape)],
        out_specs=pl.BlockSpec((None, TM, c), lambda b, i: (b, i, 0)),
        out_shape=jax.ShapeDtypeStruct((b_sz, t, c), BF16),
        scratch_shapes=[pltpu.VMEM((TM + 2 * EV_HALO, c), F32),
                        pltpu.VMEM((SUBLANES - 1, TM + 2 * EV_HALO - SUBLANES, c), F32)],
        compiler_params=_cparams(("parallel", "parallel")),
        name="even_conv_module",
    )(u, u, u, w, bias, g, bb)


def _ffn_tile(x, a1, a2, mod_ref, g_ref, wo1_ref, wo2_ref, wi_ref, wo_ref, act_ref, hid, chunk):
    o = _dot(a1, wo1_ref[...]) + _dot(a2, wo2_ref[...])
    x1 = x + mod_ref[2:3, :] * o
    h2 = _prenorm(x1, g_ref[...], mod_ref[3:4, :], mod_ref[4:5, :]).astype(BF16)
    for c in range(hid // chunk):
        gte = _dot(h2, wi_ref[:, c * chunk:(c + 1) * chunk])
        up = _dot(h2, wi_ref[:, hid + c * chunk:hid + (c + 1) * chunk])
        act_ref[:, c * chunk:(c + 1) * chunk] = (_silu(gte) * up).astype(BF16)
    return x1 + mod_ref[5:6, :] * _dot(act_ref[...], wo_ref[...])


def _out_ffn_kernel(*refs, n_src, nctx_tiles, first_tile, hid, chunk):
    a1_ref, a2_ref, mod_ref, g_ref, wo1_ref, wo2_ref, wi_ref, wo_ref, o_ref, act_ref = refs[n_src:]
    x = _load_stream(refs[:n_src], nctx_tiles, first_tile)
    o_ref[...] = _ffn_tile(x, a1_ref[...], a2_ref[...], mod_ref, g_ref, wo1_ref, wo2_ref, wi_ref, wo_ref,
                           act_ref, hid, chunk)


def _out_ffn(src, a1, a2, mod, norm_g, wo1, wo2, wi, wo, layer, nctx_tiles, latent_only):
    first = nctx_tiles if latent_only else 0
    src_specs, src_arrays, (b_sz, t, d) = _stream_specs(src, nctx_tiles, first)
    hid = wo.shape[0]
    n_tiles = t // TM - first
    mod_spec, g_spec = _row_specs(b_sz, nctx_tiles, d, layer, first)
    row = lambda n: pl.BlockSpec((None, TM, n), lambda b, i: (b, i + first, 0))
    kern = functools.partial(_out_ffn_kernel, n_src=len(src_arrays), nctx_tiles=nctx_tiles, first_tile=first,
                             hid=hid, chunk=2 * LANES)
    in_place = len(src_arrays) == 1 and not latent_only
    return pl.pallas_call(
        kern,
        grid=(b_sz, n_tiles),
        in_specs=src_specs + [row(a1.shape[-1]), row(a2.shape[-1]), mod_spec, g_spec,
                              _full(wo1.shape), _full(wo2.shape), _full(wi.shape), _full(wo.shape)],
        out_specs=pl.BlockSpec((None, TM, d), lambda b, i: (b, i, 0)),
        out_shape=jax.ShapeDtypeStruct((b_sz, n_tiles * TM, d), F32),
        scratch_shapes=[pltpu.VMEM((TM, hid), BF16)],
        input_output_aliases={0: 0} if in_place else {},
        compiler_params=_cparams(("parallel", "parallel"), resident=True),
        name="out_proj_ffn",
    )(*src_arrays, a1, a2, mod, norm_g, wo1, wo2, wi, wo)


def _od_in_kernel(x_ref, mod_ref, g_ref, w_ref, qg_ref, kg_ref, cos_ref, sin_ref, bd_ref, dtb_ref,
                  q_ref, k_ref, vt_ref, xbc_ref, z_ref, dt_ref):
    hb = _prenorm(x_ref[...], g_ref[...], mod_ref[0:1, :], mod_ref[1:2, :]).astype(BF16)
    cos = cos_ref[...]
    sin = sin_ref[...]
    bd = bd_ref[...]
    lane = lax.broadcasted_iota(jnp.int32, (1, LANES), 1)
    first = (lane % (C_HEAD_DIM // 2)) < (C_HEAD_DIM // 4)

    def norm_rope(t, g):
        hi, lo = _split2(t * t)
        ss = _dot(jnp.concatenate([hi, lo], axis=1), bd)
        y = t * lax.rsqrt(ss * (1.0 / C_HEAD_DIM) + EPS) * g
        rot = jnp.where(first, pltpu.roll(y, LANES - C_HEAD_DIM // 4, 1), pltpu.roll(y, C_HEAD_DIM // 4, 1))
        return y * cos + rot * sin

    nq = C_Q // LANES
    nk = 2 * C_KV // LANES
    qkv = _dot(hb, w_ref[:, 0:C_Q + 4 * C_KV])
    qg = qg_ref[...] * (C_HEAD_DIM ** -0.5 * LOG2E)
    for j in range(nq):
        sl = slice(j * LANES, (j + 1) * LANES)
        q_ref[:, sl] = norm_rope(qkv[:, sl], qg).astype(BF16)
    for j in range(nk):
        k_ref[:, j * LANES:(j + 1) * LANES] = norm_rope(qkv[:, C_Q + j * LANES:C_Q + (j + 1) * LANES], kg_ref[...]).astype(BF16)
    for j in range(nk):
        vt_ref[j] = qkv[:, C_Q + 2 * C_KV + j * LANES:C_Q + 2 * C_KV + (j + 1) * LANES].T.astype(BF16)
    rest = _dot(hb, w_ref[:, C_Q + 4 * C_KV:])
    xbc_ref[...] = rest[:, 0:D_XBC]
    z_ref[...] = rest[:, D_XBC:D_XBC + D_INNER]
    dtr = rest[:, D_XBC + D_INNER:] + dtb_ref[...]
    dt_ref[...] = jnp.maximum(dtr, 0.0) + jnp.log1p(jnp.exp(-jnp.abs(dtr)))


def _od_in(xs, mod, norm_g, w, qg, kg, cos, sin, bd, dtb, layer, nctx_tiles):
    b_sz, t, d = xs.shape
    x_spec = pl.BlockSpec((None, TM, d), lambda b, i: (b, i, 0))
    mod_spec, g_spec = _row_specs(b_sz, nctx_tiles, d, layer)
    tab = pl.BlockSpec((TM, LANES), lambda b, i: (i, 0))
    out = lambda n: pl.BlockSpec((None, TM, n), lambda b, i: (b, i, 0))
    return pl.pallas_call(
        _od_in_kernel,
        grid=(b_sz, t // TM),
        in_specs=[x_spec, mod_spec, g_spec, _full(w.shape), _full(qg.shape), _full(kg.shape), tab, tab,
                  _full(bd.shape), _full(dtb.shape)],
        out_specs=[out(C_Q), out(2 * C_KV),
                   pl.BlockSpec((None, C_KV_HEADS, None, LANES, TM), lambda b, i: (b, 0, i, 0, 0)),
                   out(D_XBC), out(D_INNER), out(LANES)],
        out_shape=[jax.ShapeDtypeStruct((b_sz, t, C_Q), BF16),
                   jax.ShapeDtypeStruct((b_sz, t, 2 * C_KV), BF16),
                   jax.ShapeDtypeStruct((b_sz, C_KV_HEADS, t // TM, LANES, TM), BF16),
                   jax.ShapeDtypeStruct((b_sz, t, D_XBC), F32),
                   jax.ShapeDtypeStruct((b_sz, t, D_INNER), F32),
                   jax.ShapeDtypeStruct((b_sz, t, LANES), F32)],
        compiler_params=_cparams(("parallel", "parallel"), resident=True),
        name="odd_in_proj",
    )(xs, mod, norm_g, w, qg, kg, cos, sin, bd, dtb)


def _od_attn_kernel(sink_ref, q_ref, k_ref, vt_ref, o_ref, s_ref, *, n_ctx, t, nctx_tiles):
    i = pl.program_id(1)
    tm = q_ref.shape[0]
    w = C_WINDOW
    n_tiles = vt_ref.shape[1]
    grp = C_HEADS // C_KV_HEADS
    span = tm + 2 * w
    nkeys = n_ctx + span
    prev = jnp.maximum(i - 1, 0)
    nxt = jnp.minimum(i + 1, n_tiles - 1)
    zero = jnp.zeros((), BF16)
    low_lane = lax.broadcasted_iota(jnp.int32, (1, LANES), 1) < C_HEAD_DIM
    low_row = lax.broadcasted_iota(jnp.int32, (LANES, 1), 0) < C_HEAD_DIM
    k2, vt2 = [], []
    for g in range(C_KV_HEADS):
        gl = slice(g * LANES, (g + 1) * LANES)
        k_all = jnp.concatenate([k_ref[0:n_ctx, gl],
                                 k_ref[pl.ds(pl.multiple_of(prev * tm + w, w), w), gl],
                                 k_ref[pl.ds(pl.multiple_of(i * tm, tm), tm), gl],
                                 k_ref[pl.ds(pl.multiple_of(nxt * tm, tm), w), gl]], axis=0)
        vt_all = jnp.concatenate([vt_ref[g, j] for j in range(n_ctx // tm)]
                                 + [vt_ref[g, prev][:, w:], vt_ref[g, i], vt_ref[g, nxt][:, :w]], axis=1)
        k2.append(jnp.concatenate([jnp.where(low_lane, k_all, zero), jnp.where(low_lane, zero, k_all)], axis=0))
        vt2.append(jnp.concatenate([jnp.where(low_row, vt_all, zero), jnp.where(low_row, zero, vt_all)], axis=1))
    rr = lax.broadcasted_iota(jnp.int32, (span, tm), 0)
    cc = lax.broadcasted_iota(jnp.int32, (span, tm), 1)
    kpos = i * tm - w + rr
    ok = jnp.logical_and(jnp.abs(rr - w - cc) <= w, jnp.logical_and(kpos >= n_ctx, kpos < t))
    bias = jnp.where(jnp.logical_and(ok, i >= nctx_tiles), 0.0, NEG)
    n_pairs = C_HEADS // 2

    def scores(n):
        s_ref[n % 3] = _dot_nt(k2[n // (grp // 2)], q_ref[:, n * LANES:(n + 1) * LANES])

    scores(0)
    scores(1)
    for n in range(n_pairs):
        if n + 2 < n_pairs:
            scores(n + 2)
        ps, invs = [], []
        for par in range(2):
            sk = sink_ref[2 * n + par] * LOG2E
            s_c = s_ref[n % 3, par * nkeys:par * nkeys + n_ctx, :]
            s_l = s_ref[n % 3, par * nkeys + n_ctx:(par + 1) * nkeys, :] + bias
            m = jnp.maximum(jnp.maximum(jnp.max(s_c, axis=0, keepdims=True), jnp.max(s_l, axis=0, keepdims=True)), sk)
            p_c = jnp.exp2(s_c - m)
            p_l = jnp.exp2(s_l - m)
            den = jnp.sum(p_c, axis=0, keepdims=True) + jnp.sum(p_l, axis=0, keepdims=True) + jnp.exp2(sk - m)
            ps += [p_c.astype(BF16), p_l.astype(BF16)]
            invs.append(1.0 / den)
        o_t = _dot(vt2[n // (grp // 2)], jnp.concatenate(ps, axis=0)) * jnp.where(low_row, invs[0], invs[1])
        o_ref[:, n * LANES:(n + 1) * LANES] = o_t.T.astype(BF16)


def _od_attn(sink, q, k, vt, n_ctx):
    b_sz, t, _ = q.shape
    assert TM == 2 * C_WINDOW
    kern = functools.partial(_od_attn_kernel, n_ctx=n_ctx, t=t, nctx_tiles=n_ctx // TM)
    return pl.pallas_call(
        kern,
        grid=(b_sz, t // TM),
        in_specs=[pl.BlockSpec(memory_space=pltpu.SMEM),
                  pl.BlockSpec((None, TM, C_Q), lambda b, i: (b, i, 0)),
                  pl.BlockSpec((None, t, 2 * C_KV), lambda b, i: (b, 0, 0)),
                  pl.BlockSpec((None, C_KV_HEADS, t // TM, LANES, TM), lambda b, i: (b, 0, 0, 0, 0))],
        out_specs=pl.BlockSpec((None, TM, C_Q), lambda b, i: (b, i, 0)),
        out_shape=jax.ShapeDtypeStruct((b_sz, t, C_Q), BF16),
        scratch_shapes=[pltpu.VMEM((3, 2 * (n_ctx + TM + 2 * C_WINDOW), TM), F32)],
        compiler_params=_cparams(("parallel", "arbitrary"), resident=True),
        name="odd_attention",
    )(sink, q, k, vt)


def _od_conv_kernel(u_ref, up_ref, un_ref, w_ref, b_ref, xs_ref, bc_ref, buf, *, nctx_tiles, n_tiles, rb):
    i = pl.program_id(1)
    has_prev, has_next = _halo_flags(i, nctx_tiles, n_tiles)
    tm = u_ref.shape[0]
    buf[0:OD_HALO, :] = jnp.where(has_prev, up_ref[...], 0.0)
    buf[OD_HALO:OD_HALO + tm, :] = u_ref[...]
    buf[OD_HALO + tm:, :] = jnp.where(has_next, un_ref[...], 0.0)
    base = OD_HALO - D_CONV // 2
    cw = bc_ref.shape[1]
    for r in range(tm // rb):
        rows = slice(r * rb, (r + 1) * rb)
        for cb in range(D_XBC // cw):
            cs = slice(cb * cw, (cb + 1) * cw)
            acc = jnp.zeros((rb, cw), F32) + b_ref[:, cs]
            for k in range(D_CONV):
                acc = acc + buf[pl.ds(r * rb + base + k, rb), cs] * w_ref[k:k + 1, cs]
            y = _silu(acc)
            if cb * cw < D_INNER:
                xs_ref[rows, cs] = y
            else:
                bc_ref[rows, :] = y.astype(BF16)


def _od_conv(xbc, w, bias, n_ctx):
    b_sz, t, c = xbc.shape
    n_tiles = t // TM
    kern = functools.partial(_od_conv_kernel, nctx_tiles=n_ctx // TM, n_tiles=n_tiles, rb=TM)
    out = lambda n: pl.BlockSpec((None, TM, n), lambda b, i: (b, i, 0))
    return pl.pallas_call(
        kern,
        grid=(b_sz, n_tiles),
        in_specs=_halo_specs(t, c, OD_HALO) + [_full(w.shape), _full(bias.shape)],
        out_specs=[out(D_INNER), out(c - D_INNER)],
        out_shape=[jax.ShapeDtypeStruct((b_sz, t, D_INNER), F32),
                   jax.ShapeDtypeStruct((b_sz, t, c - D_INNER), BF16)],
        scratch_shapes=[pltpu.VMEM((TM + 2 * OD_HALO, c), F32)],
        compiler_params=_cparams(("parallel", "parallel")),
        name="odd_ssd_conv",
    )(xbc, xbc, xbc, w, bias)


def _ssd_kernel(*refs, reverse, gated):
    if gated:
        xs_ref, bc_ref, dt_ref, av_ref, ex_ref, dsk_ref, y1_ref, z_ref, gn_ref, o_ref, h_ref = refs
    else:
        xs_ref, bc_ref, dt_ref, av_ref, ex_ref, dsk_ref, o_ref, h_ref = refs
    q = D_CHUNK
    gw = D_INNER // D_GROUPS
    hpg = D_HEADS // D_GROUPS
    col0 = D_HEADS if reverse else 0

    @pl.when(pl.program_id(1) == 0)
    def _():
        h_ref[...] = jnp.zeros_like(h_ref)

    ii = lax.broadcasted_iota(jnp.int32, (q, q), 0)
    jj = lax.broadcasted_iota(jnp.int32, (q, q), 1)
    tri = (jj >= ii) if reverse else (jj <= ii)
    low = lax.broadcasted_iota(jnp.int32, (1, LANES), 1) < D_HEAD_DIM
    edge = 0 if reverse else q - 1

    def expand(w):
        hi, lo = _split2(w)
        return _dot(jnp.concatenate([hi, lo], axis=1), ex_ref[...])

    rows_all = xs_ref.shape[0]
    n_chunks = rows_all // q
    ri = lax.broadcasted_iota(jnp.int32, (rows_all, rows_all), 0)
    rj = lax.broadcasted_iota(jnp.int32, (rows_all, rows_all), 1)
    same = (ri // q) == (rj // q)
    tri_all = jnp.where(jnp.logical_and(same, (rj >= ri) if reverse else (rj <= ri)), 1.0, 0.0).astype(BF16)
    dt_all = dt_ref[...]
    a = dt_all * av_ref[...]
    a1 = a.astype(BF16)
    r1 = a - a1.astype(F32)
    a2 = r1.astype(BF16)
    a3 = (r1 - a2.astype(F32)).astype(BF16)
    cs = _dot(tri_all, jnp.concatenate([a1, a2, a3], axis=1))
    acs_all = cs[:, 0:LANES] + cs[:, LANES:2 * LANES] + cs[:, 2 * LANES:3 * LANES]
    tot_all = jnp.concatenate([jnp.broadcast_to(acs_all[c * q + edge:c * q + edge + 1, :], (q, LANES))
                               for c in range(n_chunks)], axis=0)
    xs_all = xs_ref[...]
    w_y_all = expand(jnp.exp(acs_all))
    xdt_all = (xs_all * expand(dt_all)).astype(BF16)
    xw_all = (xs_all * expand(dt_all * jnp.exp(tot_all - acs_all))).astype(BF16)
    acs_t_all = acs_all.T
    for ch in (range(n_chunks - 1, -1, -1) if reverse else range(n_chunks)):
        rows = slice(ch * q, (ch + 1) * q)
        acs, acs_t = acs_all[rows, :], acs_t_all[:, rows]
        xs, w_y, xdt, xw = xs_all[rows, :], w_y_all[rows, :], xdt_all[rows, :], xw_all[rows, :]
        bc = bc_ref[rows, :]
        h_in = h_ref[...]
        hb = h_in.astype(BF16)
        ys = []
        for g in range(D_GROUPS):
            gs = slice(g * gw, (g + 1) * gw)
            b_g = bc[:, g * D_STATE:(g + 1) * D_STATE]
            c_g = bc[:, (D_GROUPS + g) * D_STATE:(D_GROUPS + g + 1) * D_STATE]
            cb = _dot_nt(c_g, b_g)
            inter = _dot(c_g, hb[:, gs]) * w_y[:, gs]
            for pr in range(hpg // 2):
                x2 = xdt[:, g * gw + pr * LANES:g * gw + (pr + 1) * LANES]
                outs = []
                for par in range(2):
                    c = col0 + g * hpg + 2 * pr + par
                    seg = acs[:, c:c + 1] - acs_t[c:c + 1, :]
                    lmat = (jnp.exp(jnp.where(tri, seg, NEG)) * cb).astype(BF16)
                    outs.append(_dot(lmat, x2))
                ys.append(jnp.where(low, outs[0], outs[1]) + inter[:, pr * LANES:(pr + 1) * LANES])
            h_ref[:, gs] = h_in[:, gs] * w_y[edge:edge + 1, gs] + _dot_tn(b_g, xw[:, gs])
        y = jnp.concatenate(ys, axis=1)
        if gated:
            gz = (y1_ref[rows, :] + y) * _silu(z_ref[rows, :])
            ms = jnp.mean(gz * gz, axis=-1, keepdims=True)
            o_ref[rows, :] = (gz * lax.rsqrt(ms + EPS) * gn_ref[...]).astype(BF16)
        else:
            o_ref[rows, :] = dsk_ref[...] * xs + y


def _ssd(xs, bc, dt, avec, ex, dsk, n_ctx, reverse, y1=None, z=None, gn=None):
    b_sz, t, _ = xs.shape
    q = TM
    nc = t // q
    ncc = n_ctx // q
    if reverse:
        cmap = lambda b, s: (b, jnp.where(s < ncc, ncc - 1 - s, nc - 1 - (s - ncc)), 0)
    else:
        cmap = lambda b, s: (b, s, 0)
    row = lambda n: pl.BlockSpec((None, q, n), cmap)
    gated = y1 is not None
    args = [xs, bc, dt, avec, ex, dsk]
    specs = [row(D_INNER), row(bc.shape[-1]), row(LANES), _full(avec.shape), _full(ex.shape), _full(dsk.shape)]
    if gated:
        args += [y1, z, gn]
        specs += [row(D_INNER), row(D_INNER), _full(gn.shape)]
    return pl.pallas_call(
        functools.partial(_ssd_kernel, reverse=reverse, gated=gated),
        grid=(b_sz, nc),
        in_specs=specs,
        out_specs=row(D_INNER),
        out_shape=jax.ShapeDtypeStruct((b_sz, t, D_INNER), BF16 if gated else F32),
        scratch_shapes=[pltpu.VMEM((D_STATE, D_INNER), F32)],
        compiler_params=_cparams(("parallel", "arbitrary")),
        name="ssd_scan_bwd_gate" if gated else "ssd_scan_fwd",
    )(*args)


def _rope_tables(seq, n_ctx, head_dim):
    pos = jnp.arange(seq)
    row = (pos // GRID_W).astype(F32)
    col = (pos % GRID_W).astype(F32)
    quarter = head_dim // 4
    inv = ROPE_THETA ** (-jnp.arange(quarter, dtype=F32) / quarter)
    ar = row[:, None] * inv
    ac = col[:, None] * inv
    ang = jnp.concatenate([ar, ar, ac, ac], axis=-1)
    sign = jnp.where((jnp.arange(head_dim) % (head_dim // 2)) < quarter, -1.0, 1.0).astype(F32)
    cos = jnp.concatenate([jnp.ones((n_ctx, head_dim), F32), jnp.cos(ang)], axis=0)
    sin = jnp.concatenate([jnp.zeros((n_ctx, head_dim), F32), jnp.sin(ang) * sign], axis=0)
    rep = LANES // head_dim
    return jnp.tile(cos, (1, rep)), jnp.tile(sin, (1, rep))


def _dup_heads(w, head_dim):
    d, n = w.shape
    w = w.reshape(d, n // head_dim, 1, head_dim)
    return jnp.broadcast_to(w, (d, n // head_dim, 2, head_dim)).reshape(d, 2 * n)


def kernel(x, c, ctx, c_ctx, mod_w, mod_b, norm1_g, norm2_g, ffn_w_in, ffn_w_out, ev_w_in, ev_w_out, ev_q_g, ev_k_g, ev_dw_w, ev_dw_b, ev_cn_g, ev_cn_b, od_w_in, od_w_out, od_q_g, od_k_g, od_sink, od_conv_w, od_conv_b, od_dt_bias, od_A_log, od_D, od_gnorm_g):
    b_sz, seq, d = x.shape
    n_ctx = ctx.shape[1]
    depth = mod_w.shape[0]
    assert seq % TM == 0 and n_ctx % TM == 0 and seq >= TM + 2 * C_WINDOW and b_sz + 1 <= MOD_ROWS
    nctx_tiles = n_ctx // TM

    xs = (ctx, x)
    cc = jnp.zeros((MOD_ROWS, d), F32).at[:b_sz].set(c).at[b_sz].set(c_ctx)
    mod = _modulation(cc, mod_w, mod_b).reshape(depth, MOD_ROWS, 6, d)
    g1 = norm1_g.reshape(depth, 1, d)
    g2 = norm2_g.reshape(depth, 1, d)

    cos_a, sin_a = _rope_tables(seq, n_ctx, A_HEAD_DIM)
    cos_c, sin_c = _rope_tables(seq, n_ctx, C_HEAD_DIM)
    lane = jnp.arange(LANES)
    bd = (lane[:, None] // C_HEAD_DIM == lane[None, :] // C_HEAD_DIM).astype(BF16)
    bd = jnp.concatenate([bd, bd], axis=0)
    head_of_lane = jnp.arange(D_INNER) // D_HEAD_DIM
    ex = []
    for dirn in range(2):
        e = (lane[:, None] == head_of_lane[None, :] + dirn * D_HEADS).astype(BF16)
        ex.append(jnp.concatenate([e, e], axis=0))

    for l in range(depth):
        i = l // 2
        wi = ffn_w_in[l].astype(BF16)
        wo = ffn_w_out[l].astype(BF16)
        if l % 2 == 0:
            w_in = ev_w_in[i].astype(BF16)
            w_out = ev_w_out[i].astype(BF16)
            q, k, v, u = _ev_in(xs, mod, g1, w_in, ev_q_g[i].reshape(1, -1), ev_k_g[i].reshape(1, -1),
                                cos_a, sin_a, l, nctx_tiles)
            a1 = _ev_attn(q, k, v, n_ctx)
            conv_p = (ev_dw_w[i], ev_dw_b[i].reshape(1, -1), ev_cn_g[i].reshape(1, -1), ev_cn_b[i].reshape(1, -1))
            wo1, wo2 = w_out[:A_Q], w_out[A_Q:]
            a2 = _ev_conv(u, *conv_p, n_ctx)
        else:
            w = od_w_in[i]
            o_k, o_v, o_x, o_dt, o_z = C_Q, C_Q + C_KV, C_Q + 2 * C_KV, C_Q + 2 * C_KV + D_XBC, C_Q + 2 * C_KV + D_XBC + 2 * D_HEADS
            w_in = jnp.concatenate([
                w[:, :o_k], _dup_heads(w[:, o_k:o_v], C_HEAD_DIM), _dup_heads(w[:, o_v:o_x], C_HEAD_DIM),
                w[:, o_x:o_dt], w[:, o_z:], w[:, o_dt:o_z], jnp.zeros((d, LANES - 2 * D_HEADS), F32)], axis=1).astype(BF16)
            w_out = od_w_out[i].astype(BF16)
            pad = jnp.zeros((LANES - 2 * D_HEADS,), F32)
            dtb = jnp.concatenate([od_dt_bias[i].reshape(-1), pad]).reshape(1, LANES)
            avec = jnp.concatenate([-jnp.exp(od_A_log[i].astype(F32)).reshape(-1), pad]).reshape(1, LANES)
            dsk = jnp.repeat(od_D[i].astype(F32), D_HEAD_DIM).reshape(1, D_INNER)
            qg = jnp.tile(od_q_g[i], LANES // C_HEAD_DIM).reshape(1, LANES)
            kg = jnp.tile(od_k_g[i], LANES // C_HEAD_DIM).reshape(1, LANES)
            q, k, v, xbc, z, dt = _od_in(xs, mod, g1, w_in, qg, kg, cos_c, sin_c, bd, dtb, l, nctx_tiles)
            a1 = _od_attn(od_sink[i], q, k, v, n_ctx)
            xc, bc = _od_conv(xbc, od_conv_w[i], od_conv_b[i].reshape(1, -1), n_ctx)
            y1 = _ssd(xc, bc, dt, avec, ex[0], dsk, n_ctx, reverse=False)
            a2 = _ssd(xc, bc, dt, avec, ex[1], dsk, n_ctx, reverse=True, y1=y1, z=z,
                      gn=od_gnorm_g[i].reshape(1, -1))
            wo1, wo2 = w_out[:C_Q], w_out[C_Q:]
        xs = _out_ffn(xs, a1, a2, mod, g2, wo1, wo2, wi, wo, l, nctx_tiles, latent_only=(l == depth - 1))
    return xs
```
